```python
import math, functools
import jax, jax.numpy as jnp
from jax import lax
import numpy as np

D_MODEL = 1024
BATCH = 4
SEQ = 8192
DEPTH = 1

N_ATT_HEADS = 8
HEAD_DIM = 64
ATT_WIDTH = N_ATT_HEADS * HEAD_DIM
CONV_WIDTH = 512
MIX_WIDTH = ATT_WIDTH + CONV_WIDTH
IN_PROJ_WIDTH = 3 * ATT_WIDTH + 3 * CONV_WIDTH
DILATED_PATTERNS = ((128, 1), (512, 4), (2048, 16))
ATT_BLOCK = 128
ROPE_THETA = 10000.0
CONV_TAPS = 3
N_EXPERTS = 32
TOP_K = 4
D_FF_EXPERT = 1024
SWIGLU_ALPHA = 1.702
SWIGLU_LIMIT = 7.0
MOE_BLOCK = 128
NORM_EPS = 1e-5

kernel_name = "hybrid_dilated_attn_shortconv_moe"


def rms_norm(x, g):
    xf = x.astype(jnp.float32)
    y = xf * lax.rsqrt(jnp.mean(xf * xf, axis=-1, keepdims=True) + NORM_EPS)
    return (y * g.astype(jnp.float32)).astype(x.dtype)


def rotary(t):
    S, hd = t.shape[1], t.shape[3]
    inv_freq = 1.0 / (ROPE_THETA ** (jnp.arange(0, hd, 2, dtype=jnp.float32) / hd))
    ang = jnp.arange(S, dtype=jnp.float32)[:, None] * inv_freq[None, :]
    cos = jnp.concatenate([jnp.cos(ang), jnp.cos(ang)], axis=-1)[None, :, None, :]
    sin = jnp.concatenate([jnp.sin(ang), jnp.sin(ang)], axis=-1)[None, :, None, :]
    tf = t.astype(jnp.float32)
    t1, t2 = tf[..., : hd // 2], tf[..., hd // 2:]
    rot = jnp.concatenate([-t2, t1], axis=-1)
    return (tf * cos + rot * sin).astype(t.dtype)


def strided_window_attention(q, k, v, dil, n_taps):
    Bsz, S, H, hd = q.shape
    BLK = ATT_BLOCK
    L = S // dil
    nb = -(-L // BLK)
    Lp = nb * BLK

    def classes(t):
        return t.reshape(Bsz, L, dil, H, hd).transpose(0, 2, 3, 1, 4)

    qc, kc, vc = classes(q), classes(k), classes(v)
    qb = jnp.pad(qc, ((0, 0), (0, 0), (0, 0), (0, Lp - L), (0, 0))).reshape(Bsz, dil, H, nb, BLK, hd)

    def band(t):
        tp = jnp.pad(t, ((0, 0), (0, 0), (0, 0), (BLK, Lp - L), (0, 0)))
        prev = tp[:, :, :, :Lp].reshape(Bsz, dil, H, nb, BLK, hd)
        cur = tp[:, :, :, BLK:].reshape(Bsz, dil, H, nb, BLK, hd)
        return jnp.concatenate([prev, cur], axis=4)

    kb, vb = band(kc), band(vc)
    s = jnp.einsum('brhnqd,brhnkd->brhnqk', qb.astype(jnp.float32), kb.astype(jnp.float32)) * (hd ** -0.5)
    qi = jnp.arange(BLK)[:, None]
    kj = jnp.arange(2 * BLK)[None, :]
    dist = qi - kj + BLK
    kpos = jnp.arange(nb)[:, None, None] * BLK - BLK + kj[None]
    valid = (dist >= 0) & (dist < n_taps) & (kpos >= 0)
    s = jnp.where(valid, s, -jnp.inf)
    lse = jax.nn.logsumexp(s, axis=-1)
    p = jnp.exp(s - lse[..., None])
    o = jnp.einsum('brhnqk,brhnkd->brhnqd', p, vb.astype(jnp.float32))
    o = o.reshape(Bsz, dil, H, Lp, hd)[:, :, :, :L].transpose(0, 3, 1, 2, 4).reshape(Bsz, S, H, hd)
    lse = lse.reshape(Bsz, dil, H, Lp)[..., :L].transpose(0, 3, 1, 2).reshape(Bsz, S, H)
    return o, lse


def hybrid_mixer(hn, w_in, attn_g, conv_w, conv_g, w_out):
    Bsz, S, _ = hn.shape
    proj = hn @ w_in
    q, k, v, u, gate_b, gate_c = jnp.split(proj, 6, axis=-1)
    q = rotary(q.reshape(Bsz, S, N_ATT_HEADS, HEAD_DIM))
    k = rotary(k.reshape(Bsz, S, N_ATT_HEADS, HEAD_DIM))
    v = v.reshape(Bsz, S, N_ATT_HEADS, HEAD_DIM)
    outs, lses = [], []
    for window, dil in DILATED_PATTERNS:
        o, l = strided_window_attention(q, k, v, dil, window // dil + 1)
        outs.append(o)
        lses.append(l)
    mix_w = jax.nn.softmax(jnp.stack(lses), axis=0)
    attn = jnp.einsum('pbsh,pbshd->bshd', mix_w, jnp.stack(outs))
    attn = attn.reshape(Bsz, S, ATT_WIDTH).astype(hn.dtype)

    z = gate_c * u
    zp = jnp.pad(z, ((0, 0), (CONV_TAPS - 1, 0), (0, 0)))
    c = zp[:, 0:S] * conv_w[0] + zp[:, 1:S + 1] * conv_w[1] + zp[:, 2:S + 2] * conv_w[2]
    conv_out = gate_b * c

    mixed = jnp.concatenate([rms_norm(attn, attn_g), rms_norm(conv_out, conv_g)], axis=-1)
    return mixed @ w_out


def moe_ffn(h, w_router, b_router, w_gu, b_gu, w_dn, b_dn):
    Bsz, S, D = h.shape
    T = Bsz * S
    BLK = MOE_BLOCK
    xt = h.reshape(T, D)
    logits = (xt @ w_router + b_router).astype(jnp.float32)
    top_val, top_idx = lax.top_k(logits, TOP_K)
    gates = jax.nn.softmax(top_val, axis=-1)
    M = T * TOP_K
    flat_e = top_idx.reshape(M)
    flat_tok = jnp.arange(M, dtype=jnp.int32) // TOP_K
    flat_g = gates.reshape(M)
    order = jnp.argsort(flat_e)
    se, stok, sg = flat_e[order], flat_tok[order], flat_g[order]
    counts = jnp.bincount(flat_e, length=N_EXPERTS)
    starts = jnp.cumsum(counts) - counts
    pcounts = (counts + BLK - 1) // BLK * BLK
    pends = jnp.cumsum(pcounts)
    pstarts = pends - pcounts
    dest = pstarts[se] + (jnp.arange(M) - starts[se])
    NB = (M + BLK - 1) // BLK + N_EXPERTS
    P = NB * BLK
    row_tok = jnp.full((P,), T, dtype=jnp.int32).at[dest].set(stok)
    row_gate = jnp.zeros((P,), jnp.float32).at[dest].set(sg)
    blk_e = jnp.minimum(jnp.searchsorted(pends, jnp.arange(NB) * BLK, side='right'), N_EXPERTS - 1)
    x_pad = jnp.concatenate([xt, jnp.zeros((1, D), xt.dtype)], axis=0)
    xb = x_pad[row_tok].reshape(NB, BLK, D)

    def expert_block(args):
        xblk, e = args
        gu = xblk @ w_gu[e] + b_gu[e]
        g, up = gu[:, :D_FF_EXPERT], gu[:, D_FF_EXPERT:]
        g = jnp.minimum(g, SWIGLU_LIMIT)
        up = jnp.clip(up, -SWIGLU_LIMIT, SWIGLU_LIMIT)
        act = (up + 1.0) * (g * jax.nn.sigmoid(SWIGLU_ALPHA * g))
        return act @ w_dn[e] + b_dn[e]

    yb = lax.map(expert_block, (xb, blk_e)).reshape(P, D)
    y = jnp.zeros((T + 1, D), h.dtype).at[row_tok].add(yb * row_gate[:, None].astype(h.dtype))
    return y[:T].reshape(Bsz, S, D)


def setup_inputs(seed: int = 0) -> dict:
    key = jax.random.key(seed)
    ks = jax.random.split(key, 16)
    f32 = jnp.float32
    def nrm(k, shape, scale):
        return jax.random.normal(k, shape, f32) * scale
    return {
        "x": nrm(ks[0], (BATCH, SEQ, D_MODEL), 1.0),
        "norm1_g": 1.0 + nrm(ks[1], (DEPTH, D_MODEL), 0.01),
        "w_mix_in": nrm(ks[2], (DEPTH, D_MODEL, IN_PROJ_WIDTH), D_MODEL ** -0.5),
        "attn_norm_g": 1.0 + nrm(ks[3], (DEPTH, ATT_WIDTH), 0.01),
        "conv_w": nrm(ks[4], (DEPTH, CONV_TAPS, CONV_WIDTH), CONV_TAPS ** -0.5),
        "conv_norm_g": 1.0 + nrm(ks[5], (DEPTH, CONV_WIDTH), 0.01),
        "w_mix_out": nrm(ks[6], (DEPTH, MIX_WIDTH, D_MODEL), MIX_WIDTH ** -0.5),
        "norm2_g": 1.0 + nrm(ks[7], (DEPTH, D_MODEL), 0.01),
        "w_router": nrm(ks[8], (DEPTH, D_MODEL, N_EXPERTS), D_MODEL ** -0.5),
        "b_router": nrm(ks[9], (DEPTH, N_EXPERTS), 0.01),
        "w_gate_up": nrm(ks[10], (DEPTH, N_EXPERTS, D_MODEL, 2 * D_FF_EXPERT), D_MODEL ** -0.5),
        "b_gate_up": nrm(ks[11], (DEPTH, N_EXPERTS, 2 * D_FF_EXPERT), 0.01),
        "w_down": nrm(ks[12], (DEPTH, N_EXPERTS, D_FF_EXPERT, D_MODEL), D_FF_EXPERT ** -0.5),
        "b_down": nrm(ks[13], (DEPTH, N_EXPERTS, D_MODEL), 0.01),
        "final_norm_g": 1.0 + nrm(ks[14], (D_MODEL,), 0.01),
    }


def reference(x, norm1_g, w_mix_in, attn_norm_g, conv_w, conv_norm_g, w_mix_out,
              norm2_g, w_router, b_router, w_gate_up, b_gate_up, w_down, b_down,
              final_norm_g):
    h = x
    for layer in range(DEPTH):
        hn = rms_norm(h, norm1_g[layer])
        h = h + hybrid_mixer(hn, w_mix_in[layer], attn_norm_g[layer], conv_w[layer],
                             conv_norm_g[layer], w_mix_out[layer])
        hn = rms_norm(h, norm2_g[layer])
        h = h + moe_ffn(hn, w_router[layer], b_router[layer], w_gate_up[layer],
                        b_gate_up[layer], w_down[layer], b_down[layer])
    return rms_norm(h, final_norm_g)
```

```python
import functools

import jax
import jax.numpy as jnp
from jax import lax
from jax.experimental import pallas as pl
from jax.experimental.pallas import tpu as pltpu

F32 = jnp.float32
BF16 = jnp.bfloat16

N_HEADS = 8
HEAD_DIM = 64
ATT_WIDTH = N_HEADS * HEAD_DIM
CONV_WIDTH = 512
DILATIONS = (1, 4, 16)
ATT_BLOCK = 128
ROPE_THETA = 10000.0
N_EXPERTS = 32
TOP_K = 4
SWIGLU_ALPHA = 1.702
SWIGLU_LIMIT = 7.0
NORM_EPS = 1e-5
NEG_BIG = -1e30

LANES = 128
VMEM_LIMIT = 52 * 1024 * 1024

PROJ_TILE = 512
ATT_QROWS = 512
EXPERT_BLOCK = 512
DISPATCH_TILE = 256
COMBINE_TILE = 128


def _rms(x, g):
    return x * lax.rsqrt(jnp.mean(x * x, axis=-1, keepdims=True) + NORM_EPS) * g


def _inproj_kernel(x_ref, g1_ref, w_ref, cos_ref, sin_ref, convw_ref, convg_ref,
                   q_ref, k_ref, v_ref, c_ref, zbuf_ref):
    tm = x_ref.shape[1]
    hn = _rms(x_ref[0], g1_ref[...]).astype(BF16)
    proj = jnp.dot(hn, w_ref[...], preferred_element_type=F32)

    cos = cos_ref[...]
    sin = sin_ref[...]
    lane = lax.broadcasted_iota(jnp.int32, (tm, LANES), 1)
    first_half = (lane % HEAD_DIM) < (HEAD_DIM // 2)

    def rope(t):
        partner = jnp.where(first_half, pltpu.roll(t, LANES - HEAD_DIM // 2, 1),
                            pltpu.roll(t, HEAD_DIM // 2, 1))
        return t * cos + partner * sin

    for c in range(ATT_WIDTH // LANES):
        sl = slice(c * LANES, (c + 1) * LANES)
        q_ref[0, :, sl] = (rope(proj[:, sl]) * (HEAD_DIM ** -0.5)).astype(BF16)
        k_ref[0, :, sl] = rope(proj[:, ATT_WIDTH + c * LANES:ATT_WIDTH + (c + 1) * LANES]).astype(BF16)
    v_ref[0] = proj[:, 2 * ATT_WIDTH:3 * ATT_WIDTH].astype(BF16)

    base = 3 * ATT_WIDTH
    u = proj[:, base:base + CONV_WIDTH]
    gate_b = proj[:, base + CONV_WIDTH:base + 2 * CONV_WIDTH]
    gate_c = proj[:, base + 2 * CONV_WIDTH:base + 3 * CONV_WIDTH]
    z = gate_c * u

    @pl.when(pl.program_id(1) == 0)
    def _():
        zbuf_ref[0:8, :] = jnp.zeros((8, CONV_WIDTH), F32)

    @pl.when(pl.program_id(1) != 0)
    def _():
        zbuf_ref[0:8, :] = zbuf_ref[tm:tm + 8, :]

    zbuf_ref[8:8 + tm, :] = z
    z1 = zbuf_ref[7:7 + tm, :]
    z2 = zbuf_ref[6:6 + tm, :]
    conv = z2 * convw_ref[0:1, :] + z1 * convw_ref[1:2, :] + z * convw_ref[2:3, :]
    c_ref[0] = _rms(gate_b * conv, convg_ref[...]).astype(BF16)


def _in_projection(x, norm1_g, w_in, conv_w, conv_g, cos_tab, sin_tab):
    B, S, D = x.shape
    tm = PROJ_TILE
    width = w_in.shape[1]
    out = jax.ShapeDtypeStruct((B, S, ATT_WIDTH), BF16)
    tok_spec = pl.BlockSpec((1, tm, ATT_WIDTH), lambda b, i: (b, i, 0))
    full = lambda shape: pl.BlockSpec(shape, lambda b, i: (0,) * len(shape))
    return pl.pallas_call(
        _inproj_kernel,
        grid=(B, S // tm),
        in_specs=[
            pl.BlockSpec((1, tm, D), lambda b, i: (b, i, 0)),
            full((1, D)),
            full((D, width)),
            pl.BlockSpec((tm, LANES), lambda b, i: (i, 0)),
            pl.BlockSpec((tm, LANES), lambda b, i: (i, 0)),
            full((3, CONV_WIDTH)),
            full((1, CONV_WIDTH)),
        ],
        out_specs=[tok_spec, tok_spec, tok_spec, tok_spec],
        out_shape=[out, out, out, out],
        scratch_shapes=[pltpu.VMEM((tm + 8, CONV_WIDTH), F32)],
        compiler_params=pltpu.CompilerParams(
            dimension_semantics=("arbitrary", "arbitrary"), vmem_limit_bytes=VMEM_LIMIT),
        name="inproj",
    )(x, norm1_g.reshape(1, D), w_in, cos_tab, sin_tab, conv_w, conv_g.reshape(1, CONV_WIDTH))


def _attn_kernel(*refs, has_prev, is_last):
    if has_prev:
        q_ref, kp_ref, kc_ref, vp_ref, vc_ref, op_ref, lp_ref = refs[:7]
        outs = refs[7:]
    else:
        q_ref, kp_ref, kc_ref, vp_ref, vc_ref = refs[:5]
        outs = refs[5:]
    o_ref = outs[0]
    lse_ref = None if is_last else outs[1]

    qrows = q_ref.shape[1]
    blk = ATT_BLOCK
    n = pl.program_id(2)

    qi = lax.broadcasted_iota(jnp.int32, (blk, 2 * blk), 0)
    kj = lax.broadcasted_iota(jnp.int32, (blk, 2 * blk), 1)
    band = (kj >= qi) & (kj <= qi + blk)
    bias_band = jnp.where(band, 0.0, NEG_BIG).astype(F32)
    bias_first = jnp.where(band & ((kj >= blk) | (n > 0)), 0.0, NEG_BIG).astype(F32)

    lane = lax.broadcasted_iota(jnp.int32, (blk, LANES), 1)
    head0 = lane < HEAD_DIM

    for j in range(qrows // blk):
        rows = slice(j * blk, (j + 1) * blk)
        bias = bias_first if j == 0 else bias_band
        for c in range(ATT_WIDTH // LANES):
            cols = slice(c * LANES, (c + 1) * LANES)
            q2 = q_ref[0, rows, cols]
            if j == 0:
                k2 = jnp.concatenate([kp_ref[0, :, cols], kc_ref[0, 0:blk, cols]], axis=0)
                v2 = jnp.concatenate([vp_ref[0, :, cols], vc_ref[0, 0:blk, cols]], axis=0)
            else:
                k2 = kc_ref[0, (j - 1) * blk:(j + 1) * blk, cols]
                v2 = vc_ref[0, (j - 1) * blk:(j + 1) * blk, cols]
            o_pair = None
            lse_pair = None
            for hh in range(2):
                sel = head0 if hh == 0 else jnp.logical_not(head0)
                qm = jnp.where(sel, q2, jnp.zeros_like(q2))
                s = lax.dot_general(qm, k2, (((1,), (1,)), ((), ())),
                                    preferred_element_type=F32) + bias
                m = jnp.max(s, axis=-1, keepdims=True)
                p = jnp.exp(s - m)
                l = jnp.sum(p, axis=-1, keepdims=True)
                o = jnp.dot(p.astype(BF16), v2, preferred_element_type=F32) / l
                lse = m + jnp.log(l)
                if hh == 0:
                    o_pair, lse_pair = o, jnp.broadcast_to(lse, (blk, LANES))
                else:
                    o_pair = jnp.where(head0, o_pair, o)
                    lse_pair = jnp.where(head0, lse_pair, jnp.broadcast_to(lse, (blk, LANES)))
            if has_prev:
                o_old = op_ref[0, rows, cols]
                l_old = lp_ref[0, rows, cols]
                mx = jnp.maximum(l_old, lse_pair)
                wa = jnp.exp(l_old - mx)
                wb = jnp.exp(lse_pair - mx)
                o_pair = (wa * o_old + wb * o_pair) / (wa + wb)
                lse_pair = mx + jnp.log(wa + wb)
            o_ref[0, rows, cols] = o_pair.astype(o_ref.dtype)
            if lse_ref is not None:
                lse_ref[0, rows, cols] = lse_pair


def _attention_pattern(q, k, v, dil, prev, is_last):
    B, S, W = q.shape
    L = S // dil
    qrows = min(ATT_QROWS, L)
    nq = L // qrows
    per_q = qrows // ATT_BLOCK
    view = lambda t: t.reshape(B, L, dil * W)

    cur = pl.BlockSpec((1, qrows, W), lambda b, r, n: (b, n, r))
    prv = pl.BlockSpec((1, ATT_BLOCK, W), lambda b, r, n: (b, jnp.maximum(n * per_q - 1, 0), r))
    in_specs = [cur, prv, cur, prv, cur]
    args = [view(q), view(k), view(k), view(v), view(v)]
    if prev is not None:
        in_specs += [cur, cur]
        args += [view(prev[0]), view(prev[1])]
    f32_out = jax.ShapeDtypeStruct((B, L, dil * W), F32)
    out_shape = [f32_out] if is_last else [f32_out, f32_out]
    out_specs = [cur] if is_last else [cur, cur]
    res = pl.pallas_call(
        functools.partial(_attn_kernel, has_prev=prev is not None, is_last=is_last),
        grid=(B, dil, nq),
        in_specs=in_specs,
        out_specs=out_specs,
        out_shape=out_shape,
        compiler_params=pltpu.CompilerParams(
            dimension_semantics=("arbitrary", "arbitrary", "arbitrary"),
            vmem_limit_bytes=VMEM_LIMIT),
        name=f"dilated_attn_d{dil}",
    )(*args)
    return [t.reshape(B, S, W) for t in res]


def _split3(t):
    hi = t.astype(BF16)
    r1 = t - hi.astype(F32)
    mid = r1.astype(BF16)
    lo = (r1 - mid.astype(F32)).astype(BF16)
    return hi, mid, lo


def _outproj_router_kernel(attn_ref, conv_ref, x_ref, ga_ref, wo_ref, g2_ref, wr_ref, br_ref,
                           h_ref, hn_ref, idx_ref, gate_ref, rank_ref, cnt_ref, carry_ref):
    tm = x_ref.shape[0]
    step = pl.program_id(0)

    @pl.when(step == 0)
    def _():
        carry_ref[...] = jnp.zeros_like(carry_ref)

    attn_n = _rms(attn_ref[...], ga_ref[...]).astype(BF16)
    mixed = jnp.dot(attn_n, wo_ref[0:ATT_WIDTH, :], preferred_element_type=F32)
    mixed += jnp.dot(conv_ref[...], wo_ref[ATT_WIDTH:, :], preferred_element_type=F32)
    h = x_ref[...] + mixed
    h_ref[...] = h
    hn = _rms(h, g2_ref[...])
    hn_ref[...] = hn

    a0, a1, a2 = _split3(hn)
    w0, w1, w2 = (wr_ref[0], wr_ref[1], wr_ref[2])
    dot = lambda a, w: jnp.dot(a, w, preferred_element_type=F32)
    logits = (dot(a0, w0) + (dot(a0, w1) + dot(a1, w0))
              + (dot(a0, w2) + dot(a1, w1) + dot(a2, w0))) + br_ref[...]

    eidx = lax.broadcasted_iota(jnp.int32, (tm, N_EXPERTS), 1)
    work = logits
    vals, idxs = [], []
    for _ in range(TOP_K):
        mval = jnp.max(work, axis=-1, keepdims=True)
        midx = jnp.min(jnp.where(work == mval, eidx, N_EXPERTS), axis=-1, keepdims=True)
        vals.append(mval)
        idxs.append(midx)
        work = jnp.where(eidx == midx, -jnp.inf, work)
    exps = [jnp.exp(v - vals[0]) for v in vals]
    denom = exps[0] + exps[1] + exps[2] + exps[3]

    chosen = jnp.zeros((tm, N_EXPERTS), F32)
    for midx in idxs:
        chosen = chosen + (eidx == midx).astype(F32)
    ri = lax.broadcasted_iota(jnp.int32, (tm, tm), 0)
    ci = lax.broadcasted_iota(jnp.int32, (tm, tm), 1)
    strict_lower = (ci < ri).astype(BF16)
    before = jnp.dot(strict_lower, chosen.astype(BF16), preferred_element_type=F32) + carry_ref[...]
    for kk in range(TOP_K):
        idx_ref[:, kk:kk + 1] = idxs[kk]
        gate_ref[:, kk:kk + 1] = exps[kk] / denom
        rank_ref[:, kk:kk + 1] = jnp.sum(
            jnp.where(eidx == idxs[kk], before, 0.0), axis=-1, keepdims=True).astype(jnp.int32)
    carry_ref[...] = carry_ref[...] + jnp.sum(chosen, axis=0, keepdims=True)
    cnt_ref[...] = carry_ref[...].astype(jnp.int32)


def _out_projection_router(attn, conv_n, x, attn_g, w_out, norm2_g, w_router, b_router):
    T, D = x.shape
    tm = PROJ_TILE
    tok = lambda width: pl.BlockSpec((tm, width), lambda i: (i, 0))
    full = lambda shape: pl.BlockSpec(shape, lambda i: (0,) * len(shape))
    wr_pieces = jnp.stack(_split3(w_router))
    return pl.pallas_call(
        _outproj_router_kernel,
        grid=(T // tm,),
        in_specs=[tok(ATT_WIDTH), tok(CONV_WIDTH), tok(D), full((1, ATT_WIDTH)),
                  full((ATT_WIDTH + CONV_WIDTH, D)), full((1, D)),
                  full((3, D, N_EXPERTS)), full((1, N_EXPERTS))],
        out_specs=[tok(D), tok(D), tok(TOP_K), tok(TOP_K), tok(TOP_K), full((1, N_EXPERTS))],
        out_shape=[jax.ShapeDtypeStruct((T, D), F32), jax.ShapeDtypeStruct((T, D), F32),
                   jax.ShapeDtypeStruct((T, TOP_K), jnp.int32),
                   jax.ShapeDtypeStruct((T, TOP_K), F32),
                   jax.ShapeDtypeStruct((T, TOP_K), jnp.int32),
                   jax.ShapeDtypeStruct((1, N_EXPERTS), jnp.int32)],
        scratch_shapes=[pltpu.VMEM((1, N_EXPERTS), F32)],
        compiler_params=pltpu.CompilerParams(
            dimension_semantics=("arbitrary",), vmem_limit_bytes=VMEM_LIMIT),
        name="outproj_router",
    )(attn, conv_n, x, attn_g.reshape(1, -1), w_out, norm2_g.reshape(1, D),
      wr_pieces, b_router.reshape(1, N_EXPERTS))


def _dispatch_kernel(dest_ref, hn_ref, xb_ref, sem):
    tm = hn_ref.shape[0]

    def row_copy(t, kk):
        return pltpu.make_async_copy(
            hn_ref.at[pl.ds(t, 1)], xb_ref.at[pl.ds(dest_ref[t * TOP_K + kk], 1)], sem)

    def issue(t, carry):
        for kk in range(TOP_K):
            row_copy(t, kk).start()
        return carry

    lax.fori_loop(0, tm, issue, 0)
    for _ in range(TOP_K):
        pltpu.make_async_copy(hn_ref, xb_ref.at[pl.ds(0, tm)], sem).wait()


def _dispatch(hn, dest_flat, n_rows):
    T, D = hn.shape
    tm = DISPATCH_TILE
    return pl.pallas_call(
        _dispatch_kernel,
        grid=(T // tm,),
        in_specs=[pl.BlockSpec((tm * TOP_K,), lambda i: (i,), memory_space=pltpu.SMEM),
                  pl.BlockSpec((tm, D), lambda i: (i, 0))],
        out_specs=pl.BlockSpec(memory_space=pl.ANY),
        out_shape=jax.ShapeDtypeStruct((n_rows, D), F32),
        scratch_shapes=[pltpu.SemaphoreType.DMA(())],
        compiler_params=pltpu.CompilerParams(
            dimension_semantics=("arbitrary",), has_side_effects=True),
        name="moe_dispatch",
    )(dest_flat, hn)


def _expert_kernel(vblk_ref, vexp_ref, vfirst_ref, gstart_ref, nvis_ref,
                   x_ref, wgu_ref, bgu_ref, wdn_ref, bdn_ref, y_ref):
    bm = x_ref.shape[0]
    ff = wdn_ref.shape[1]
    vis = pl.program_id(0)

    @pl.when(vis < nvis_ref[0])
    def _():
        e = vexp_ref[vis]
        row0 = vblk_ref[vis] * bm
        row = lax.broadcasted_iota(jnp.int32, (bm, 1), 0) + row0
        mine = (row >= gstart_ref[e]) & (row < gstart_ref[e + 1])
        xb = x_ref[...].astype(BF16)
        gu = jnp.dot(xb, wgu_ref[0], preferred_element_type=F32) + bgu_ref[0]
        g = jnp.minimum(gu[:, :ff], SWIGLU_LIMIT)
        up = jnp.clip(gu[:, ff:], -SWIGLU_LIMIT, SWIGLU_LIMIT)
        act = (up + 1.0) * (g * jax.nn.sigmoid(SWIGLU_ALPHA * g))
        y = jnp.dot(act.astype(BF16), wdn_ref[0], preferred_element_type=F32) + bdn_ref[0]

        @pl.when(vfirst_ref[vis] == 1)
        def _():
            y_ref[...] = jnp.where(mine, y, 0.0)

        @pl.when(vfirst_ref[vis] == 0)
        def _():
            y_ref[...] = jnp.where(mine, y, y_ref[...])


def _experts(xb, visits, w_gu, b_gu, w_dn, b_dn):
    M, D = xb.shape
    E, _, ff2 = w_gu.shape
    bm = EXPERT_BLOCK
    vblk, vexp, vfirst, gstart, nvis = visits
    row_map = lambda v, vb, ve, vf, gs, nv: (vb[v], 0)
    exp_map = lambda v, vb, ve, vf, gs, nv: (ve[v], 0, 0)
    grid_spec = pltpu.PrefetchScalarGridSpec(
        num_scalar_prefetch=5,
        grid=(vblk.shape[0],),
        in_specs=[
            pl.BlockSpec((bm, D), row_map),
            pl.BlockSpec((1, D, ff2), exp_map),
            pl.BlockSpec((1, 1, ff2), exp_map),
            pl.BlockSpec((1, ff2 // 2, D), exp_map),
            pl.BlockSpec((1, 1, D), exp_map),
        ],
        out_specs=pl.BlockSpec((bm, D), row_map),
    )
    return pl.pallas_call(
        _expert_kernel,
        grid_spec=grid_spec,
        out_shape=jax.ShapeDtypeStruct((M, D), F32),
        compiler_params=pltpu.CompilerParams(
            dimension_semantics=("arbitrary",), vmem_limit_bytes=VMEM_LIMIT),
        name="moe_experts",
    )(vblk, vexp, vfirst, gstart, nvis, xb, w_gu, b_gu.reshape(E, 1, ff2), w_dn,
      b_dn.reshape(E, 1, D))


def _visit_schedule(counts, n_rows, bm):
    n_blocks = n_rows // bm
    max_visits = n_blocks + N_EXPERTS - 1
    ends = jnp.cumsum(counts)
    starts = ends - counts
    first_blk = starts // bm
    last_blk = jnp.maximum(ends - 1, 0) // bm
    n_vis = jnp.where(counts > 0, last_blk - first_blk + 1, 0)
    vis_end = jnp.cumsum(n_vis)
    vis_start = vis_end - n_vis
    total = vis_end[-1]
    v = jnp.minimum(jnp.arange(max_visits, dtype=jnp.int32), total - 1)
    vexp = jnp.sum(v[:, None] >= vis_end[None, :], axis=-1).astype(jnp.int32)
    pick = lambda table: jnp.sum(
        jnp.where(vexp[:, None] == jnp.arange(N_EXPERTS, dtype=jnp.int32), table[None, :], 0), axis=-1)
    vblk = (pick(first_blk) + v - pick(vis_start)).astype(jnp.int32)
    vfirst = jnp.concatenate([jnp.ones((1,), jnp.int32),
                              (vblk[1:] != vblk[:-1]).astype(jnp.int32)])
    gstart = jnp.concatenate([starts, ends[-1:]]).astype(jnp.int32)
    return vblk, vexp, vfirst, gstart, total.astype(jnp.int32).reshape(1)


def _combine_kernel(dest_ref, yb_ref, gate_ref, h_ref, gf_ref, out_ref, buf_ref, sem):
    tm = h_ref.shape[0]

    def row_copy(t, kk):
        return pltpu.make_async_copy(
            yb_ref.at[pl.ds(dest_ref[t * TOP_K + kk], 1)], buf_ref.at[kk, pl.ds(t, 1)], sem)

    def issue(t, carry):
        for kk in range(TOP_K):
            row_copy(t, kk).start()
        return carry

    lax.fori_loop(0, tm, issue, 0)
    for kk in range(TOP_K):
        pltpu.make_async_copy(yb_ref.at[pl.ds(0, tm)], buf_ref.at[kk], sem).wait()

    gates = gate_ref[...]
    y = h_ref[...]
    for kk in range(TOP_K):
        y = y + gates[:, kk:kk + 1] * buf_ref[kk]
    out_ref[...] = _rms(y, gf_ref[...])


def _combine(yb, dest_flat, gates, h, final_g):
    T, D = h.shape
    tm = COMBINE_TILE
    return pl.pallas_call(
        _combine_kernel,
        grid=(T // tm,),
        in_specs=[pl.BlockSpec((tm * TOP_K,), lambda i: (i,), memory_space=pltpu.SMEM),
                  pl.BlockSpec(memory_space=pl.ANY),
                  pl.BlockSpec((tm, TOP_K), lambda i: (i, 0)),
                  pl.BlockSpec((tm, D), lambda i: (i, 0)),
                  pl.BlockSpec((1, D), lambda i: (0, 0))],
        out_specs=pl.BlockSpec((tm, D), lambda i: (i, 0)),
        out_shape=jax.ShapeDtypeStruct((T, D), F32),
        scratch_shapes=[pltpu.VMEM((TOP_K, tm, D), F32), pltpu.SemaphoreType.DMA(())],
        compiler_params=pltpu.CompilerParams(dimension_semantics=("arbitrary",)),
        name="moe_combine",
    )(dest_flat, yb, gates, h, final_g.reshape(1, D))


def _rope_tables(seq_len):
    half = HEAD_DIM // 2
    inv_freq = 1.0 / (ROPE_THETA ** (jnp.arange(0, HEAD_DIM, 2, dtype=F32) / HEAD_DIM))
    ang = jnp.arange(seq_len, dtype=F32)[:, None] * inv_freq[None, :]
    cos = jnp.tile(jnp.cos(ang), (1, LANES // half))
    sin = jnp.tile(jnp.concatenate([-jnp.sin(ang), jnp.sin(ang)], axis=-1), (1, LANES // HEAD_DIM))
    return cos, sin


def _moe(hn, h, idx, gates, rank, counts, w_gu, b_gu, w_dn, b_dn, final_g):
    T, D = hn.shape
    n_assign = T * TOP_K
    counts = counts.reshape(N_EXPERTS)
    starts = jnp.cumsum(counts) - counts
    onehot = idx[..., None] == jnp.arange(N_EXPERTS, dtype=jnp.int32)
    dest = rank + jnp.sum(jnp.where(onehot, starts, 0), axis=-1)
    dest_flat = dest.reshape(n_assign).astype(jnp.int32)
    visits = _visit_schedule(counts, n_assign, EXPERT_BLOCK)

    xb = _dispatch(hn, dest_flat, n_assign)
    yb = _experts(xb, visits, w_gu.astype(BF16), b_gu, w_dn.astype(BF16), b_dn)
    return _combine(yb, dest_flat, gates, h, final_g)


def kernel(x, norm1_g, w_mix_in, attn_norm_g, conv_w, conv_norm_g, w_mix_out, norm2_g,
           w_router, b_router, w_gate_up, b_gate_up, w_down, b_down, final_norm_g):
    B, S, D = x.shape
    assert norm1_g.shape[0] == 1, "single-layer block"
    cos_tab, sin_tab = _rope_tables(S)
    q, k, v, conv_n = _in_projection(x, norm1_g[0], w_mix_in[0].astype(BF16), conv_w[0],
                                     conv_norm_g[0], cos_tab, sin_tab)
    prev = None
    for p, dil in enumerate(DILATIONS):
        is_last = p == len(DILATIONS) - 1
        prev = _attention_pattern(q, k, v, dil, prev, is_last)
    attn = prev[0]

    T = B * S
    h, hn, idx, gates, rank, counts = _out_projection_router(
        attn.reshape(T, ATT_WIDTH), conv_n.reshape(T, CONV_WIDTH), x.reshape(T, D),
        attn_norm_g[0], w_mix_out[0].astype(BF16), norm2_g[0], w_router[0], b_router[0])
    out = _moe(hn, h, idx, gates, rank, counts, w_gate_up[0], b_gate_up[0], w_down[0],
               b_down[0], final_norm_g)
    return out.reshape(B, S, D)
```

```python
import functools

import jax
import jax.numpy as jnp
from jax import lax
from jax.experimental import pallas as pl
from jax.experimental.pallas import tpu as pltpu

F32 = jnp.float32
BF16 = jnp.bfloat16

N_HEADS = 8
HEAD_DIM = 64
ATT_WIDTH = N_HEADS * HEAD_DIM
CONV_WIDTH = 512
ATT_BLOCK = 128
ROPE_THETA = 10000.0
N_EXPERTS = 32
TOP_K = 4
SWIGLU_ALPHA = 1.702
SWIGLU_LIMIT = 7.0
NORM_EPS = 1e-5
NEG_BIG = -1e30

LANES = 128
N_PAIRS = ATT_WIDTH // LANES
VMEM_LIMIT = 52 * 1024 * 1024

PROJ_TILE = 512
ATT_ROWS = 512
EXPERT_BLOCK = 512
DISPATCH_TILE = 256
COMBINE_TILE = 128


def _rms(x, g):
    return x * lax.rsqrt(jnp.mean(x * x, axis=-1, keepdims=True) + NORM_EPS) * g


def _inproj_kernel(x_ref, g1_ref, w_ref, cos_ref, sin_ref, convw_ref, convg_ref,
                   nat_ref, d4_ref, d16_ref, c_ref, zbuf_ref, sa_ref, sb_ref):
    tm = x_ref.shape[1]
    hn = _rms(x_ref[0], g1_ref[...]).astype(BF16)
    proj = jnp.dot(hn, w_ref[...], preferred_element_type=F32)

    cos = cos_ref[...]
    sin = sin_ref[...]
    lane = lax.broadcasted_iota(jnp.int32, (tm, LANES), 1)
    first_half = (lane % HEAD_DIM) < (HEAD_DIM // 2)

    def rope(t):
        partner = jnp.where(first_half, pltpu.roll(t, LANES - HEAD_DIM // 2, 1),
                            pltpu.roll(t, HEAD_DIM // 2, 1))
        return t * cos + partner * sin

    for ti in range(3):
        for c in range(N_PAIRS):
            col = ti * ATT_WIDTH + c * LANES
            val = proj[:, col:col + LANES]
            if ti == 0:
                val = rope(val) * (HEAD_DIM ** -0.5)
            elif ti == 1:
                val = rope(val)
            nat_ref[ti, c] = val.astype(BF16)
            slab = ti * N_PAIRS + c
            sa_ref[slab] = val
            for r in range(4):
                piece = sa_ref[slab, pl.ds(r, tm // 4, stride=4), :]
                d4_ref[ti, c, r] = piece.astype(BF16)
                sb_ref[slab, r] = piece
            for r4 in range(4):
                for j in range(4):
                    piece = sb_ref[slab, r4, pl.ds(j, tm // 16, stride=4), :]
                    d16_ref[ti, c, r4 + 4 * j] = piece.astype(BF16)

    base = 3 * ATT_WIDTH
    u = proj[:, base:base + CONV_WIDTH]
    gate_b = proj[:, base + CONV_WIDTH:base + 2 * CONV_WIDTH]
    gate_c = proj[:, base + 2 * CONV_WIDTH:base + 3 * CONV_WIDTH]
    z = gate_c * u

    @pl.when(pl.program_id(1) == 0)
    def _():
        zbuf_ref[0:8, :] = jnp.zeros((8, CONV_WIDTH), F32)

    @pl.when(pl.program_id(1) != 0)
    def _():
        zbuf_ref[0:8, :] = zbuf_ref[tm:tm + 8, :]

    zbuf_ref[8:8 + tm, :] = z
    z1 = zbuf_ref[7:7 + tm, :]
    z2 = zbuf_ref[6:6 + tm, :]
    conv = z2 * convw_ref[0:1, :] + z1 * convw_ref[1:2, :] + z * convw_ref[2:3, :]
    c_ref[0] = _rms(gate_b * conv, convg_ref[...]).astype(BF16)


def _in_projection(x, norm1_g, w_in, conv_w, conv_g, cos_tab, sin_tab):
    B, S, D = x.shape
    tm = PROJ_TILE
    width = w_in.shape[1]
    full = lambda shape: pl.BlockSpec(shape, lambda b, i: (0,) * len(shape))
    nat = jax.ShapeDtypeStruct((3, N_PAIRS, B, S, LANES), BF16)
    d4 = jax.ShapeDtypeStruct((3, N_PAIRS, B, 4, S // 4, LANES), BF16)
    d16 = jax.ShapeDtypeStruct((3, N_PAIRS, B, 16, S // 16, LANES), BF16)
    return pl.pallas_call(
        _inproj_kernel,
        grid=(B, S // tm),
        in_specs=[
            pl.BlockSpec((1, tm, D), lambda b, i: (b, i, 0)),
            full((1, D)),
            full((D, width)),
            pl.BlockSpec((tm, LANES), lambda b, i: (i, 0)),
            pl.BlockSpec((tm, LANES), lambda b, i: (i, 0)),
            full((3, CONV_WIDTH)),
            full((1, CONV_WIDTH)),
        ],
        out_specs=[
            pl.BlockSpec((3, N_PAIRS, None, tm, LANES), lambda b, i: (0, 0, b, i, 0)),
            pl.BlockSpec((3, N_PAIRS, None, 4, tm // 4, LANES), lambda b, i: (0, 0, b, 0, i, 0)),
            pl.BlockSpec((3, N_PAIRS, None, 16, tm // 16, LANES), lambda b, i: (0, 0, b, 0, i, 0)),
            pl.BlockSpec((1, tm, CONV_WIDTH), lambda b, i: (b, i, 0)),
        ],
        out_shape=[nat, d4, d16, jax.ShapeDtypeStruct((B, S, CONV_WIDTH), BF16)],
        scratch_shapes=[pltpu.VMEM((tm + 8, CONV_WIDTH), F32),
                        pltpu.VMEM((3 * N_PAIRS, tm, LANES), F32),
                        pltpu.VMEM((3 * N_PAIRS, 4, tm // 4, LANES), F32)],
        compiler_params=pltpu.CompilerParams(
            dimension_semantics=("arbitrary", "arbitrary"), vmem_limit_bytes=VMEM_LIMIT),
        name="inproj",
    )(x, norm1_g.reshape(1, D), w_in, cos_tab, sin_tab, conv_w, conv_g.reshape(1, CONV_WIDTH))


def _band_bias(is_first):
    blk = ATT_BLOCK
    qi = lax.broadcasted_iota(jnp.int32, (blk, 2 * blk), 0)
    kj = lax.broadcasted_iota(jnp.int32, (blk, 2 * blk), 1)
    band = (kj >= qi) & (kj <= qi + blk)
    if is_first is not None:
        band = band & ((kj >= blk) | jnp.logical_not(is_first))
    return jnp.where(band, 0.0, NEG_BIG).astype(F32)


def _pair_attention(q2, k2, v2, bias, head0):
    o_pair = lse_pair = None
    for hh in range(2):
        sel = head0 if hh == 0 else jnp.logical_not(head0)
        qm = jnp.where(sel, q2, jnp.zeros_like(q2))
        s = lax.dot_general(qm, k2, (((1,), (1,)), ((), ())), preferred_element_type=F32) + bias
        m = jnp.max(s, axis=-1, keepdims=True)
        p = jnp.exp(s - m)
        l = jnp.sum(p, axis=-1, keepdims=True)
        o = jnp.dot(p.astype(BF16), v2, preferred_element_type=F32) / l
        lse = jnp.broadcast_to(m + jnp.log(l), o.shape)
        if hh == 0:
            o_pair, lse_pair = o, lse
        else:
            o_pair = jnp.where(head0, o_pair, o)
            lse_pair = jnp.where(head0, lse_pair, lse)
    return o_pair, lse_pair


def _merge(o_old, l_old, o_new, l_new):
    mx = jnp.maximum(l_old, l_new)
    wa = jnp.exp(l_old - mx)
    wb = jnp.exp(l_new - mx)
    return (wa * o_old + wb * o_new) / (wa + wb), mx + jnp.log(wa + wb)


def _attn_classes_kernel(*refs, has_prev):
    if has_prev:
        q_ref, kp_ref, kc_ref, vp_ref, vc_ref, op_ref, lp_ref, o_ref, lse_ref = refs
    else:
        q_ref, kp_ref, kc_ref, vp_ref, vc_ref, o_ref, lse_ref = refs
    blk = ATT_BLOCK
    bias = _band_bias(pl.program_id(1 if has_prev else 2) == 0)
    head0 = lax.broadcasted_iota(jnp.int32, (blk, LANES), 1) < HEAD_DIM
    for cls in range(4):
        for c in range(N_PAIRS):
            k2 = jnp.concatenate([kp_ref[c, cls], kc_ref[c, cls]], axis=0)
            v2 = jnp.concatenate([vp_ref[c, cls], vc_ref[c, cls]], axis=0)
            o_pair, lse_pair = _pair_attention(q_ref[c, cls], k2, v2, bias, head0)
            if has_prev:
                o_pair, lse_pair = _merge(op_ref[c, cls], lp_ref[c, cls], o_pair, lse_pair)
            o_ref[c, pl.ds(cls, blk, stride=4), :] = o_pair
            lse_ref[c, pl.ds(cls, blk, stride=4), :] = lse_pair


def _attn_natural_kernel(q_ref, kp_ref, kc_ref, vp_ref, vc_ref, op_ref, lp_ref, o_ref):
    blk = ATT_BLOCK
    rows_total = q_ref.shape[1]
    bias_band = _band_bias(None)
    bias_first = _band_bias(pl.program_id(1) == 0)
    head0 = lax.broadcasted_iota(jnp.int32, (blk, LANES), 1) < HEAD_DIM
    for j in range(rows_total // blk):
        rows = slice(j * blk, (j + 1) * blk)
        for c in range(N_PAIRS):
            if j == 0:
                k2 = jnp.concatenate([kp_ref[c], kc_ref[c, 0:blk]], axis=0)
                v2 = jnp.concatenate([vp_ref[c], vc_ref[c, 0:blk]], axis=0)
            else:
                k2 = kc_ref[c, (j - 1) * blk:(j + 1) * blk]
                v2 = vc_ref[c, (j - 1) * blk:(j + 1) * blk]
            o_pair, lse_pair = _pair_attention(q_ref[c, rows], k2, v2,
                                               bias_first if j == 0 else bias_band, head0)
            o_pair, _ = _merge(op_ref[c, rows], lp_ref[c, rows], o_pair, lse_pair)
            o_ref[rows, c * LANES:(c + 1) * LANES] = o_pair


def _dilated_attention(nat, d4, d16):
    _, _, B, S, _ = nat.shape
    blk = ATT_BLOCK
    cparams = lambda n: pltpu.CompilerParams(
        dimension_semantics=("arbitrary",) * n, vmem_limit_bytes=VMEM_LIMIT)

    L16, L4 = S // 16, S // 4
    d16v = d16.reshape(3, N_PAIRS, B, 4, 4, L16, LANES)
    def spec16(t, prev):
        row = (lambda n: jnp.maximum(n - 1, 0)) if prev else (lambda n: n)
        return pl.BlockSpec((None, N_PAIRS, None, 4, None, blk, LANES),
                            lambda b, r4, n: (t, 0, b, 0, r4, row(n), 0))
    state4 = jax.ShapeDtypeStruct((N_PAIRS, B, 4, L4, LANES), F32)
    out16 = pl.BlockSpec((N_PAIRS, None, None, 4 * blk, LANES), lambda b, r4, n: (0, b, r4, n, 0))
    o4, l4 = pl.pallas_call(
        functools.partial(_attn_classes_kernel, has_prev=False),
        grid=(B, 4, L16 // blk),
        in_specs=[spec16(0, False), spec16(1, True), spec16(1, False), spec16(2, True),
                  spec16(2, False)],
        out_specs=[out16, out16],
        out_shape=[state4, state4],
        compiler_params=cparams(3),
        name="dilated_attn_d16",
    )(d16v, d16v, d16v, d16v, d16v)

    def spec4(t, prev):
        row = (lambda n: jnp.maximum(n - 1, 0)) if prev else (lambda n: n)
        return pl.BlockSpec((None, N_PAIRS, None, 4, blk, LANES),
                            lambda b, n: (t, 0, b, 0, row(n), 0))
    in_state4 = pl.BlockSpec((N_PAIRS, None, 4, blk, LANES), lambda b, n: (0, b, 0, n, 0))
    state1 = jax.ShapeDtypeStruct((N_PAIRS, B, S, LANES), F32)
    out4 = pl.BlockSpec((N_PAIRS, None, 4 * blk, LANES), lambda b, n: (0, b, n, 0))
    o1, l1 = pl.pallas_call(
        functools.partial(_attn_classes_kernel, has_prev=True),
        grid=(B, L4 // blk),
        in_specs=[spec4(0, False), spec4(1, True), spec4(1, False), spec4(2, True),
                  spec4(2, False), in_state4, in_state4],
        out_specs=[out4, out4],
        out_shape=[state1, state1],
        compiler_params=cparams(2),
        name="dilated_attn_d4",
    )(d4, d4, d4, d4, d4, o4, l4)

    rows = ATT_ROWS
    per = rows // blk
    cur1 = lambda t: pl.BlockSpec((None, N_PAIRS, None, rows, LANES), lambda b, n: (t, 0, b, n, 0))
    prev1 = lambda t: pl.BlockSpec((None, N_PAIRS, None, blk, LANES),
                                   lambda b, n: (t, 0, b, jnp.maximum(n * per - 1, 0), 0))
    in_state1 = pl.BlockSpec((N_PAIRS, None, rows, LANES), lambda b, n: (0, b, n, 0))
    return pl.pallas_call(
        _attn_natural_kernel,
        grid=(B, S // rows),
        in_specs=[cur1(0), prev1(1), cur1(1), prev1(2), cur1(2), in_state1, in_state1],
        out_specs=pl.BlockSpec((None, rows, ATT_WIDTH), lambda b, n: (b, n, 0)),
        out_shape=jax.ShapeDtypeStruct((B, S, ATT_WIDTH), F32),
        compiler_params=cparams(2),
        name="dilated_attn_d1",
    )(nat, nat, nat, nat, nat, o1, l1)


def _split3(t):
    hi = t.astype(BF16)
    r1 = t - hi.astype(F32)
    mid = r1.astype(BF16)
    lo = (r1 - mid.astype(F32)).astype(BF16)
    return hi, mid, lo


def _outproj_router_kernel(attn_ref, conv_ref, x_ref, ga_ref, wo_ref, g2_ref, wr_ref, br_ref,
                           h_ref, hn_ref, idx_ref, gate_ref, rank_ref, cnt_ref, carry_ref):
    tm = x_ref.shape[0]
    step = pl.program_id(0)

    @pl.when(step == 0)
    def _():
        carry_ref[...] = jnp.zeros_like(carry_ref)

    attn_n = _rms(attn_ref[...], ga_ref[...]).astype(BF16)
    mixed = jnp.dot(attn_n, wo_ref[0:ATT_WIDTH, :], preferred_element_type=F32)
    mixed += jnp.dot(conv_ref[...], wo_ref[ATT_WIDTH:, :], preferred_element_type=F32)
    h = x_ref[...] + mixed
    h_ref[...] = h
    hn = _rms(h, g2_ref[...])
    hn_ref[...] = hn

    a0, a1, a2 = _split3(hn)
    w0, w1, w2 = (wr_ref[0], wr_ref[1], wr_ref[2])
    dot = lambda a, w: jnp.dot(a, w, preferred_element_type=F32)
    logits = (dot(a0, w0) + (dot(a0, w1) + dot(a1, w0))
              + (dot(a0, w2) + dot(a1, w1) + dot(a2, w0))) + br_ref[...]

    eidx = lax.broadcasted_iota(jnp.int32, (tm, N_EXPERTS), 1)
    work = logits
    vals, idxs = [], []
    for _ in range(TOP_K):
        mval = jnp.max(work, axis=-1, keepdims=True)
        midx = jnp.min(jnp.where(work == mval, eidx, N_EXPERTS), axis=-1, keepdims=True)
        vals.append(mval)
        idxs.append(midx)
        work = jnp.where(eidx == midx, -jnp.inf, work)
    exps = [jnp.exp(v - vals[0]) for v in vals]
    denom = exps[0] + exps[1] + exps[2] + exps[3]

    chosen = jnp.zeros((tm, N_EXPERTS), F32)
    for midx in idxs:
        chosen = chosen + (eidx == midx).astype(F32)
    ri = lax.broadcasted_iota(jnp.int32, (tm, tm), 0)
    ci = lax.broadcasted_iota(jnp.int32, (tm, tm), 1)
    strict_lower = (ci < ri).astype(BF16)
    before = jnp.dot(strict_lower, chosen.astype(BF16), preferred_element_type=F32) + carry_ref[...]
    for kk in range(TOP_K):
        idx_ref[:, kk:kk + 1] = idxs[kk]
        gate_ref[:, kk:kk + 1] = exps[kk] / denom
        rank_ref[:, kk:kk + 1] = jnp.sum(
            jnp.where(eidx == idxs[kk], before, 0.0), axis=-1, keepdims=True).astype(jnp.int32)
    carry_ref[...] = carry_ref[...] + jnp.sum(chosen, axis=0, keepdims=True)
    cnt_ref[...] = carry_ref[...].astype(jnp.int32)


def _out_projection_router(attn, conv_n, x, attn_g, w_out, norm2_g, w_router, b_router):
    T, D = x.shape
    tm = PROJ_TILE
    tok = lambda width: pl.BlockSpec((tm, width), lambda i: (i, 0))
    full = lambda shape: pl.BlockSpec(shape, lambda i: (0,) * len(shape))
    wr_pieces = jnp.stack(_split3(w_router))
    return pl.pallas_call(
        _outproj_router_kernel,
        grid=(T // tm,),
        in_specs=[tok(ATT_WIDTH), tok(CONV_WIDTH), tok(D), full((1, ATT_WIDTH)),
                  full((ATT_WIDTH + CONV_WIDTH, D)), full((1, D)),
                  full((3, D, N_EXPERTS)), full((1, N_EXPERTS))],
        out_specs=[tok(D), tok(D), tok(TOP_K), tok(TOP_K), tok(TOP_K), full((1, N_EXPERTS))],
        out_shape=[jax.ShapeDtypeStruct((T, D), F32), jax.ShapeDtypeStruct((T, D), F32),
                   jax.ShapeDtypeStruct((T, TOP_K), jnp.int32),
                   jax.ShapeDtypeStruct((T, TOP_K), F32),
                   jax.ShapeDtypeStruct((T, TOP_K), jnp.int32),
                   jax.ShapeDtypeStruct((1, N_EXPERTS), jnp.int32)],
        scratch_shapes=[pltpu.VMEM((1, N_EXPERTS), F32)],
        compiler_params=pltpu.CompilerParams(
            dimension_semantics=("arbitrary",), vmem_limit_bytes=VMEM_LIMIT),
        name="outproj_router",
    )(attn, conv_n, x, attn_g.reshape(1, -1), w_out, norm2_g.reshape(1, D),
      wr_pieces, b_router.reshape(1, N_EXPERTS))


def _dispatch_kernel(dest_ref, hn_ref, xb_ref, sem):
    tm = hn_ref.shape[0]

    def row_copy(t, kk):
        return pltpu.make_async_copy(
            hn_ref.at[pl.ds(t, 1)], xb_ref.at[pl.ds(dest_ref[t * TOP_K + kk], 1)], sem)

    def issue(t, carry):
        for kk in range(TOP_K):
            row_copy(t, kk).start(priority=kk % 2)
        return carry

    lax.fori_loop(0, tm, issue, 0)
    for _ in range(TOP_K):
        pltpu.make_async_copy(hn_ref, xb_ref.at[pl.ds(0, tm)], sem).wait()


def _dispatch(hn, dest_flat, n_rows):
    T, D = hn.shape
    tm = DISPATCH_TILE
    return pl.pallas_call(
        _dispatch_kernel,
        grid=(T // tm,),
        in_specs=[pl.BlockSpec((tm * TOP_K,), lambda i: (i,), memory_space=pltpu.SMEM),
                  pl.BlockSpec((tm, D), lambda i: (i, 0))],
        out_specs=pl.BlockSpec(memory_space=pl.ANY),
        out_shape=jax.ShapeDtypeStruct((n_rows, D), F32),
        scratch_shapes=[pltpu.SemaphoreType.DMA(())],
        compiler_params=pltpu.CompilerParams(
            dimension_semantics=("arbitrary",), has_side_effects=True),
        name="moe_dispatch",
    )(dest_flat, hn)


def _expert_kernel(vblk_ref, vexp_ref, vfirst_ref, gstart_ref, nvis_ref,
                   x_ref, wgu_ref, bgu_ref, wdn_ref, bdn_ref, y_ref):
    bm = x_ref.shape[0]
    ff = wdn_ref.shape[1]
    vis = pl.program_id(0)

    @pl.when(vis < nvis_ref[0])
    def _():
        e = vexp_ref[vis]
        row0 = vblk_ref[vis] * bm
        row = lax.broadcasted_iota(jnp.int32, (bm, 1), 0) + row0
        mine = (row >= gstart_ref[e]) & (row < gstart_ref[e + 1])
        xb = x_ref[...].astype(BF16)
        gu = jnp.dot(xb, wgu_ref[0], preferred_element_type=F32) + bgu_ref[0]
        g = jnp.minimum(gu[:, :ff], SWIGLU_LIMIT)
        up = jnp.clip(gu[:, ff:], -SWIGLU_LIMIT, SWIGLU_LIMIT)
        act = (up + 1.0) * (g * jax.nn.sigmoid(SWIGLU_ALPHA * g))
        y = jnp.dot(act.astype(BF16), wdn_ref[0], preferred_element_type=F32) + bdn_ref[0]

        @pl.when(vfirst_ref[vis] == 1)
        def _():
            y_ref[...] = jnp.where(mine, y, 0.0)

        @pl.when(vfirst_ref[vis] == 0)
        def _():
            y_ref[...] = jnp.where(mine, y, y_ref[...])


def _experts(xb, visits, w_gu, b_gu, w_dn, b_dn):
    M, D = xb.shape
    E, _, ff2 = w_gu.shape
    bm = EXPERT_BLOCK
    vblk, vexp, vfirst, gstart, nvis = visits
    row_map = lambda v, vb, ve, vf, gs, nv: (vb[v], 0)
    exp_map = lambda v, vb, ve, vf, gs, nv: (ve[v], 0, 0)
    grid_spec = pltpu.PrefetchScalarGridSpec(
        num_scalar_prefetch=5,
        grid=(vblk.shape[0],),
        in_specs=[
            pl.BlockSpec((bm, D), row_map),
            pl.BlockSpec((1, D, ff2), exp_map),
            pl.BlockSpec((1, 1, ff2), exp_map),
            pl.BlockSpec((1, ff2 // 2, D), exp_map),
            pl.BlockSpec((1, 1, D), exp_map),
        ],
        out_specs=pl.BlockSpec((bm, D), row_map),
    )
    return pl.pallas_call(
        _expert_kernel,
        grid_spec=grid_spec,
        out_shape=jax.ShapeDtypeStruct((M, D), F32),
        compiler_params=pltpu.CompilerParams(
            dimension_semantics=("arbitrary",), vmem_limit_bytes=VMEM_LIMIT),
        name="moe_experts",
    )(vblk, vexp, vfirst, gstart, nvis, xb, w_gu, b_gu.reshape(E, 1, ff2), w_dn,
      b_dn.reshape(E, 1, D))


def _visit_schedule(counts, n_rows, bm):
    n_blocks = n_rows // bm
    max_visits = n_blocks + N_EXPERTS - 1
    ends = jnp.cumsum(counts)
    starts = ends - counts
    first_blk = starts // bm
    last_blk = jnp.maximum(ends - 1, 0) // bm
    n_vis = jnp.where(counts > 0, last_blk - first_blk + 1, 0)
    vis_end = jnp.cumsum(n_vis)
    vis_start = vis_end - n_vis
    total = vis_end[-1]
    v = jnp.minimum(jnp.arange(max_visits, dtype=jnp.int32), total - 1)
    vexp = jnp.sum(v[:, None] >= vis_end[None, :], axis=-1).astype(jnp.int32)
    pick = lambda table: jnp.sum(
        jnp.where(vexp[:, None] == jnp.arange(N_EXPERTS, dtype=jnp.int32), table[None, :], 0), axis=-1)
    vblk = (pick(first_blk) + v - pick(vis_start)).astype(jnp.int32)
    vfirst = jnp.concatenate([jnp.ones((1,), jnp.int32),
                              (vblk[1:] != vblk[:-1]).astype(jnp.int32)])
    gstart = jnp.concatenate([starts, ends[-1:]]).astype(jnp.int32)
    return vblk, vexp, vfirst, gstart, total.astype(jnp.int32).reshape(1)


def _combine_kernel(dest_ref, yb_ref, gate_ref, h_ref, gf_ref, out_ref, buf_ref, sem):
    tm = h_ref.shape[0]

    def row_copy(t, kk):
        return pltpu.make_async_copy(
            yb_ref.at[pl.ds(dest_ref[t * TOP_K + kk], 1)], buf_ref.at[kk, pl.ds(t, 1)], sem)

    def issue(t, carry):
        for kk in range(TOP_K):
            row_copy(t, kk).start(priority=kk % 2)
        return carry

    lax.fori_loop(0, tm, issue, 0)
    for kk in range(TOP_K):
        pltpu.make_async_copy(yb_ref.at[pl.ds(0, tm)], buf_ref.at[kk], sem).wait()

    gates = gate_ref[...]
    y = h_ref[...]
    for kk in range(TOP_K):
        y = y + gates[:, kk:kk + 1] * buf_ref[kk]
    out_ref[...] = _rms(y, gf_ref[...])


def _combine(yb, dest_flat, gates, h, final_g):
    T, D = h.shape
    tm = COMBINE_TILE
    return pl.pallas_call(
        _combine_kernel,
        grid=(T // tm,),
        in_specs=[pl.BlockSpec((tm * TOP_K,), lambda i: (i,), memory_space=pltpu.SMEM),
                  pl.BlockSpec(memory_space=pl.ANY),
                  pl.BlockSpec((tm, TOP_K), lambda i: (i, 0)),
                  pl.BlockSpec((tm, D), lambda i: (i, 0)),
                  pl.BlockSpec((1, D), lambda i: (0, 0))],
        out_specs=pl.BlockSpec((tm, D), lambda i: (i, 0)),
        out_shape=jax.ShapeDtypeStruct((T, D), F32),
        scratch_shapes=[pltpu.VMEM((TOP_K, tm, D), F32), pltpu.SemaphoreType.DMA(())],
        compiler_params=pltpu.CompilerParams(dimension_semantics=("arbitrary",)),
        name="moe_combine",
    )(dest_flat, yb, gates, h, final_g.reshape(1, D))


def _rope_tables(seq_len):
    half = HEAD_DIM // 2
    inv_freq = 1.0 / (ROPE_THETA ** (jnp.arange(0, HEAD_DIM, 2, dtype=F32) / HEAD_DIM))
    ang = jnp.arange(seq_len, dtype=F32)[:, None] * inv_freq[None, :]
    cos = jnp.tile(jnp.cos(ang), (1, LANES // half))
    sin = jnp.tile(jnp.concatenate([-jnp.sin(ang), jnp.sin(ang)], axis=-1), (1, LANES // HEAD_DIM))
    return cos, sin


def _moe(hn, h, idx, gates, rank, counts, w_gu, b_gu, w_dn, b_dn, final_g):
    T, D = hn.shape
    n_assign = T * TOP_K
    counts = counts.reshape(N_EXPERTS)
    starts = jnp.cumsum(counts) - counts
    onehot = idx[..., None] == jnp.arange(N_EXPERTS, dtype=jnp.int32)
    dest = rank + jnp.sum(jnp.where(onehot, starts, 0), axis=-1)
    dest_flat = dest.reshape(n_assign).astype(jnp.int32)
    visits = _visit_schedule(counts, n_assign, EXPERT_BLOCK)

    xb = _dispatch(hn, dest_flat, n_assign)
    yb = _experts(xb, visits, w_gu.astype(BF16), b_gu, w_dn.astype(BF16), b_dn)
    return _combine(yb, dest_flat, gates, h, final_g)


def kernel(x, norm1_g, w_mix_in, attn_norm_g, conv_w, conv_norm_g, w_mix_out, norm2_g,
           w_router, b_router, w_gate_up, b_gate_up, w_down, b_down, final_norm_g):
    B, S, D = x.shape
    assert norm1_g.shape[0] == 1, "single-layer block"
    cos_tab, sin_tab = _rope_tables(S)
    nat, d4, d16, conv_n = _in_projection(x, norm1_g[0], w_mix_in[0].astype(BF16), conv_w[0],
                                          conv_norm_g[0], cos_tab, sin_tab)
    attn = _dilated_attention(nat, d4, d16)

    T = B * S
    h, hn, idx, gates, rank, counts = _out_projection_router(
        attn.reshape(T, ATT_WIDTH), conv_n.reshape(T, CONV_WIDTH), x.reshape(T, D),
        attn_norm_g[0], w_mix_out[0].astype(BF16), norm2_g[0], w_router[0], b_router[0])
    out = _moe(hn, h, idx, gates, rank, counts, w_gate_up[0], b_gate_up[0], w_down[0],
               b_down[0], final_norm_g)
    return out.reshape(B, S, D)
```

```python
import functools

import jax
import jax.numpy as jnp
from jax import lax
from jax.experimental import pallas as pl
from jax.experimental.pallas import tpu as pltpu

F32 = jnp.float32
BF16 = jnp.bfloat16

N_HEADS = 8
HEAD_DIM = 64
ATT_WIDTH = N_HEADS * HEAD_DIM
CONV_WIDTH = 512
ATT_BLOCK = 128
ROPE_THETA = 10000.0
N_EXPERTS = 32
TOP_K = 4
SWIGLU_ALPHA = 1.702
SWIGLU_LIMIT = 7.0
NORM_EPS = 1e-5
NEG_BIG = -1e30

LANES = 128
N_PAIRS = ATT_WIDTH // LANES
VMEM_LIMIT = 52 * 1024 * 1024

PROJ_TILE = 512
ATT_ROWS = 512
EXPERT_BLOCK = 512
DISPATCH_TILE = 256
COMBINE_TILE = 128


def _rms(x, g):
    return x * lax.rsqrt(jnp.mean(x * x, axis=-1, keepdims=True) + NORM_EPS) * g


def _inproj_kernel(x_ref, g1_ref, w_ref, cos_ref, sin_ref, convw_ref, convg_ref,
                   nat_ref, d4_ref, d16_ref, c_ref, zbuf_ref, sa_ref, sb_ref):
    tm = x_ref.shape[1]
    hn = _rms(x_ref[0], g1_ref[...]).astype(BF16)
    proj = jnp.dot(hn, w_ref[...], preferred_element_type=F32)

    cos = cos_ref[...]
    sin = sin_ref[...]
    lane = lax.broadcasted_iota(jnp.int32, (tm, LANES), 1)
    first_half = (lane % HEAD_DIM) < (HEAD_DIM // 2)

    def rope(t):
        partner = jnp.where(first_half, pltpu.roll(t, LANES - HEAD_DIM // 2, 1),
                            pltpu.roll(t, HEAD_DIM // 2, 1))
        return t * cos + partner * sin

    for ti in range(3):
        for c in range(N_PAIRS):
            col = ti * ATT_WIDTH + c * LANES
            val = proj[:, col:col + LANES]
            if ti == 0:
                val = rope(val) * (HEAD_DIM ** -0.5)
            elif ti == 1:
                val = rope(val)
            nat_ref[ti, c] = val.astype(BF16)
            slab = ti * N_PAIRS + c
            sa_ref[slab] = val
            for r in range(4):
                piece = sa_ref[slab, pl.ds(r, tm // 4, stride=4), :]
                d4_ref[ti, c, r] = piece.astype(BF16)
                sb_ref[slab, r] = piece
            for r4 in range(4):
                for j in range(4):
                    piece = sb_ref[slab, r4, pl.ds(j, tm // 16, stride=4), :]
                    d16_ref[ti, c, r4 + 4 * j] = piece.astype(BF16)

    base = 3 * ATT_WIDTH
    u = proj[:, base:base + CONV_WIDTH]
    gate_b = proj[:, base + CONV_WIDTH:base + 2 * CONV_WIDTH]
    gate_c = proj[:, base + 2 * CONV_WIDTH:base + 3 * CONV_WIDTH]
    z = gate_c * u

    @pl.when(pl.program_id(1) == 0)
    def _():
        zbuf_ref[0:8, :] = jnp.zeros((8, CONV_WIDTH), F32)

    @pl.when(pl.program_id(1) != 0)
    def _():
        zbuf_ref[0:8, :] = zbuf_ref[tm:tm + 8, :]

    zbuf_ref[8:8 + tm, :] = z
    z1 = zbuf_ref[7:7 + tm, :]
    z2 = zbuf_ref[6:6 + tm, :]
    conv = z2 * convw_ref[0:1, :] + z1 * convw_ref[1:2, :] + z * convw_ref[2:3, :]
    c_ref[0] = _rms(gate_b * conv, convg_ref[...]).astype(BF16)


def _in_projection(x, norm1_g, w_in, conv_w, conv_g, cos_tab, sin_tab):
    B, S, D = x.shape
    tm = PROJ_TILE
    width = w_in.shape[1]
    full = lambda shape: pl.BlockSpec(shape, lambda b, i: (0,) * len(shape))
    nat = jax.ShapeDtypeStruct((3, N_PAIRS, B, S, LANES), BF16)
    d4 = jax.ShapeDtypeStruct((3, N_PAIRS, B, 4, S // 4, LANES), BF16)
    d16 = jax.ShapeDtypeStruct((3, N_PAIRS, B, 16, S // 16, LANES), BF16)
    return pl.pallas_call(
        _inproj_kernel,
        grid=(B, S // tm),
        in_specs=[
            pl.BlockSpec((1, tm, D), lambda b, i: (b, i, 0)),
            full((1, D)),
            full((D, width)),
            pl.BlockSpec((tm, LANES), lambda b, i: (i, 0)),
            pl.BlockSpec((tm, LANES), lambda b, i: (i, 0)),
            full((3, CONV_WIDTH)),
            full((1, CONV_WIDTH)),
        ],
        out_specs=[
            pl.BlockSpec((3, N_PAIRS, None, tm, LANES), lambda b, i: (0, 0, b, i, 0)),
            pl.BlockSpec((3, N_PAIRS, None, 4, tm // 4, LANES), lambda b, i: (0, 0, b, 0, i, 0)),
            pl.BlockSpec((3, N_PAIRS, None, 16, tm // 16, LANES), lambda b, i: (0, 0, b, 0, i, 0)),
            pl.BlockSpec((1, tm, CONV_WIDTH), lambda b, i: (b, i, 0)),
        ],
        out_shape=[nat, d4, d16, jax.ShapeDtypeStruct((B, S, CONV_WIDTH), BF16)],
        scratch_shapes=[pltpu.VMEM((tm + 8, CONV_WIDTH), F32),
                        pltpu.VMEM((3 * N_PAIRS, tm, LANES), F32),
                        pltpu.VMEM((3 * N_PAIRS, 4, tm // 4, LANES), F32)],
        compiler_params=pltpu.CompilerParams(
            dimension_semantics=("arbitrary", "arbitrary"), vmem_limit_bytes=VMEM_LIMIT),
        name="inproj",
    )(x, norm1_g.reshape(1, D), w_in, cos_tab, sin_tab, conv_w, conv_g.reshape(1, CONV_WIDTH))


def _band_bias(is_first):
    blk = ATT_BLOCK
    qi = lax.broadcasted_iota(jnp.int32, (blk, 2 * blk), 0)
    kj = lax.broadcasted_iota(jnp.int32, (blk, 2 * blk), 1)
    band = (kj >= qi) & (kj <= qi + blk)
    if is_first is not None:
        band = band & ((kj >= blk) | jnp.logical_not(is_first))
    return jnp.where(band, 0.0, NEG_BIG).astype(F32)


def _pair_attention(q2, k2, v2, bias, head0):
    o_pair = lse_pair = None
    for hh in range(2):
        sel = head0 if hh == 0 else jnp.logical_not(head0)
        qm = jnp.where(sel, q2, jnp.zeros_like(q2))
        s = lax.dot_general(qm, k2, (((1,), (1,)), ((), ())), preferred_element_type=F32) + bias
        m = jnp.max(s, axis=-1, keepdims=True)
        p = jnp.exp(s - m)
        l = jnp.sum(p, axis=-1, keepdims=True)
        o = jnp.dot(p.astype(BF16), v2, preferred_element_type=F32) / l
        lse = jnp.broadcast_to(m + jnp.log(l), o.shape)
        if hh == 0:
            o_pair, lse_pair = o, lse
        else:
            o_pair = jnp.where(head0, o_pair, o)
            lse_pair = jnp.where(head0, lse_pair, lse)
    return o_pair, lse_pair


def _merge(o_old, l_old, o_new, l_new):
    mx = jnp.maximum(l_old, l_new)
    wa = jnp.exp(l_old - mx)
    wb = jnp.exp(l_new - mx)
    return (wa * o_old + wb * o_new) / (wa + wb), mx + jnp.log(wa + wb)


def _attn_classes_kernel(*refs, has_prev):
    if has_prev:
        q_ref, kp_ref, kc_ref, vp_ref, vc_ref, op_ref, lp_ref, o_ref, lse_ref = refs
    else:
        q_ref, kp_ref, kc_ref, vp_ref, vc_ref, o_ref, lse_ref = refs
    blk = ATT_BLOCK
    bias = _band_bias(pl.program_id(1 if has_prev else 2) == 0)
    head0 = lax.broadcasted_iota(jnp.int32, (blk, LANES), 1) < HEAD_DIM
    for cls in range(4):
        for c in range(N_PAIRS):
            k2 = jnp.concatenate([kp_ref[c, cls], kc_ref[c, cls]], axis=0)
            v2 = jnp.concatenate([vp_ref[c, cls], vc_ref[c, cls]], axis=0)
            o_pair, lse_pair = _pair_attention(q_ref[c, cls], k2, v2, bias, head0)
            if has_prev:
                o_pair, lse_pair = _merge(op_ref[c, cls], lp_ref[c, cls], o_pair, lse_pair)
            o_ref[c, pl.ds(cls, blk, stride=4), :] = o_pair
            lse_ref[c, pl.ds(cls, blk, stride=4), :] = lse_pair


def _attn_natural_kernel(q_ref, kp_ref, kc_ref, vp_ref, vc_ref, op_ref, lp_ref, o_ref):
    blk = ATT_BLOCK
    rows_total = q_ref.shape[1]
    bias_band = _band_bias(None)
    bias_first = _band_bias(pl.program_id(1) == 0)
    head0 = lax.broadcasted_iota(jnp.int32, (blk, LANES), 1) < HEAD_DIM
    for j in range(rows_total // blk):
        rows = slice(j * blk, (j + 1) * blk)
        for c in range(N_PAIRS):
            if j == 0:
                k2 = jnp.concatenate([kp_ref[c], kc_ref[c, 0:blk]], axis=0)
                v2 = jnp.concatenate([vp_ref[c], vc_ref[c, 0:blk]], axis=0)
            else:
                k2 = kc_ref[c, (j - 1) * blk:(j + 1) * blk]
                v2 = vc_ref[c, (j - 1) * blk:(j + 1) * blk]
            o_pair, lse_pair = _pair_attention(q_ref[c, rows], k2, v2,
                                               bias_first if j == 0 else bias_band, head0)
            o_pair, _ = _merge(op_ref[c, rows], lp_ref[c, rows], o_pair, lse_pair)
            o_ref[rows, c * LANES:(c + 1) * LANES] = o_pair


def _dilated_attention(nat, d4, d16):
    _, _, B, S, _ = nat.shape
    blk = ATT_BLOCK
    cparams = lambda n: pltpu.CompilerParams(
        dimension_semantics=("arbitrary",) * n, vmem_limit_bytes=VMEM_LIMIT)

    L16, L4 = S // 16, S // 4
    d16v = d16.reshape(3, N_PAIRS, B, 4, 4, L16, LANES)
    def spec16(t, prev):
        row = (lambda n: jnp.maximum(n - 1, 0)) if prev else (lambda n: n)
        return pl.BlockSpec((None, N_PAIRS, None, 4, None, blk, LANES),
                            lambda b, r4, n: (t, 0, b, 0, r4, row(n), 0))
    state4 = jax.ShapeDtypeStruct((N_PAIRS, B, 4, L4, LANES), F32)
    out16 = pl.BlockSpec((N_PAIRS, None, None, 4 * blk, LANES), lambda b, r4, n: (0, b, r4, n, 0))
    o4, l4 = pl.pallas_call(
        functools.partial(_attn_classes_kernel, has_prev=False),
        grid=(B, 4, L16 // blk),
        in_specs=[spec16(0, False), spec16(1, True), spec16(1, False), spec16(2, True),
                  spec16(2, False)],
        out_specs=[out16, out16],
        out_shape=[state4, state4],
        compiler_params=cparams(3),
        name="dilated_attn_d16",
    )(d16v, d16v, d16v, d16v, d16v)

    def spec4(t, prev):
        row = (lambda n: jnp.maximum(n - 1, 0)) if prev else (lambda n: n)
        return pl.BlockSpec((None, N_PAIRS, None, 4, blk, LANES),
                            lambda b, n: (t, 0, b, 0, row(n), 0))
    in_state4 = pl.BlockSpec((N_PAIRS, None, 4, blk, LANES), lambda b, n: (0, b, 0, n, 0))
    state1 = jax.ShapeDtypeStruct((N_PAIRS, B, S, LANES), F32)
    out4 = pl.BlockSpec((N_PAIRS, None, 4 * blk, LANES), lambda b, n: (0, b, n, 0))
    o1, l1 = pl.pallas_call(
        functools.partial(_attn_classes_kernel, has_prev=True),
        grid=(B, L4 // blk),
        in_specs=[spec4(0, False), spec4(1, True), spec4(1, False), spec4(2, True),
                  spec4(2, False), in_state4, in_state4],
        out_specs=[out4, out4],
        out_shape=[state1, state1],
        compiler_params=cparams(2),
        name="dilated_attn_d4",
    )(d4, d4, d4, d4, d4, o4, l4)

    rows = ATT_ROWS
    per = rows // blk
    cur1 = lambda t: pl.BlockSpec((None, N_PAIRS, None, rows, LANES), lambda b, n: (t, 0, b, n, 0))
    prev1 = lambda t: pl.BlockSpec((None, N_PAIRS, None, blk, LANES),
                                   lambda b, n: (t, 0, b, jnp.maximum(n * per - 1, 0), 0))
    in_state1 = pl.BlockSpec((N_PAIRS, None, rows, LANES), lambda b, n: (0, b, n, 0))
    return pl.pallas_call(
        _attn_natural_kernel,
        grid=(B, S // rows),
        in_specs=[cur1(0), prev1(1), cur1(1), prev1(2), cur1(2), in_state1, in_state1],
        out_specs=pl.BlockSpec((None, rows, ATT_WIDTH), lambda b, n: (b, n, 0)),
        out_shape=jax.ShapeDtypeStruct((B, S, ATT_WIDTH), F32),
        compiler_params=cparams(2),
        name="dilated_attn_d1",
    )(nat, nat, nat, nat, nat, o1, l1)


def _split3(t):
    hi = t.astype(BF16)
    r1 = t - hi.astype(F32)
    mid = r1.astype(BF16)
    lo = (r1 - mid.astype(F32)).astype(BF16)
    return hi, mid, lo


def _outproj_router_kernel(attn_ref, conv_ref, x_ref, ga_ref, wo_ref, g2_ref, wr_ref, br_ref,
                           h_ref, hn_ref, idx_ref, gate_ref, rank_ref, cnt_ref, carry_ref):
    tm = x_ref.shape[0]
    step = pl.program_id(0)

    @pl.when(step == 0)
    def _():
        carry_ref[...] = jnp.zeros_like(carry_ref)

    attn_n = _rms(attn_ref[...], ga_ref[...]).astype(BF16)
    mixed = jnp.dot(attn_n, wo_ref[0:ATT_WIDTH, :], preferred_element_type=F32)
    mixed += jnp.dot(conv_ref[...], wo_ref[ATT_WIDTH:, :], preferred_element_type=F32)
    h = x_ref[...] + mixed
    h_ref[...] = h
    hn = _rms(h, g2_ref[...])
    hn_ref[...] = hn

    a0, a1, a2 = _split3(hn)
    w0, w1, w2 = (wr_ref[0], wr_ref[1], wr_ref[2])
    dot = lambda a, w: jnp.dot(a, w, preferred_element_type=F32)
    logits = (dot(a0, w0) + (dot(a0, w1) + dot(a1, w0))
              + (dot(a0, w2) + dot(a1, w1) + dot(a2, w0))) + br_ref[...]

    eidx = lax.broadcasted_iota(jnp.int32, (tm, N_EXPERTS), 1)
    work = logits
    vals, idxs = [], []
    for _ in range(TOP_K):
        mval = jnp.max(work, axis=-1, keepdims=True)
        midx = jnp.min(jnp.where(work == mval, eidx, N_EXPERTS), axis=-1, keepdims=True)
        vals.append(mval)
        idxs.append(midx)
        work = jnp.where(eidx == midx, -jnp.inf, work)
    exps = [jnp.exp(v - vals[0]) for v in vals]
    denom = exps[0] + exps[1] + exps[2] + exps[3]

    chosen = jnp.zeros((tm, N_EXPERTS), F32)
    for midx in idxs:
        chosen = chosen + (eidx == midx).astype(F32)
    ri = lax.broadcasted_iota(jnp.int32, (tm, tm), 0)
    ci = lax.broadcasted_iota(jnp.int32, (tm, tm), 1)
    strict_lower = (ci < ri).astype(BF16)
    before = jnp.dot(strict_lower, chosen.astype(BF16), preferred_element_type=F32) + carry_ref[...]
    for kk in range(TOP_K):
        idx_ref[:, kk:kk + 1] = idxs[kk]
        gate_ref[:, kk:kk + 1] = exps[kk] / denom
        rank_ref[:, kk:kk + 1] = jnp.sum(
            jnp.where(eidx == idxs[kk], before, 0.0), axis=-1, keepdims=True).astype(jnp.int32)
    carry_ref[...] = carry_ref[...] + jnp.sum(chosen, axis=0, keepdims=True)
    cnt_ref[...] = carry_ref[...].astype(jnp.int32)


def _out_projection_router(attn, conv_n, x, attn_g, w_out, norm2_g, w_router, b_router):
    T, D = x.shape
    tm = PROJ_TILE
    tok = lambda width: pl.BlockSpec((tm, width), lambda i: (i, 0))
    full = lambda shape: pl.BlockSpec(shape, lambda i: (0,) * len(shape))
    wr_pieces = jnp.stack(_split3(w_router))
    return pl.pallas_call(
        _outproj_router_kernel,
        grid=(T // tm,),
        in_specs=[tok(ATT_WIDTH), tok(CONV_WIDTH), tok(D), full((1, ATT_WIDTH)),
                  full((ATT_WIDTH + CONV_WIDTH, D)), full((1, D)),
                  full((3, D, N_EXPERTS)), full((1, N_EXPERTS))],
        out_specs=[tok(D), tok(D), tok(TOP_K), tok(TOP_K), tok(TOP_K), full((1, N_EXPERTS))],
        out_shape=[jax.ShapeDtypeStruct((T, D), F32), jax.ShapeDtypeStruct((T, D), F32),
                   jax.ShapeDtypeStruct((T, TOP_K), jnp.int32),
                   jax.ShapeDtypeStruct((T, TOP_K), F32),
                   jax.ShapeDtypeStruct((T, TOP_K), jnp.int32),
                   jax.ShapeDtypeStruct((1, N_EXPERTS), jnp.int32)],
        scratch_shapes=[pltpu.VMEM((1, N_EXPERTS), F32)],
        compiler_params=pltpu.CompilerParams(
            dimension_semantics=("arbitrary",), vmem_limit_bytes=VMEM_LIMIT),
        name="outproj_router",
    )(attn, conv_n, x, attn_g.reshape(1, -1), w_out, norm2_g.reshape(1, D),
      wr_pieces, b_router.reshape(1, N_EXPERTS))


def _dispatch_kernel(dest_ref, hn_ref, xb_ref, sem):
    tm = hn_ref.shape[0]

    for t in range(tm):
        for kk in range(TOP_K):
            pltpu.make_async_copy(
                hn_ref.at[pl.ds(t, 1)], xb_ref.at[pl.ds(dest_ref[t * TOP_K + kk], 1)], sem
            ).start(priority=kk % 2)
    for _ in range(TOP_K):
        pltpu.make_async_copy(hn_ref, xb_ref.at[pl.ds(0, tm)], sem).wait()


def _dispatch(hn, dest_flat, n_rows):
    T, D = hn.shape
    tm = DISPATCH_TILE
    return pl.pallas_call(
        _dispatch_kernel,
        grid=(T // tm,),
        in_specs=[pl.BlockSpec((tm * TOP_K,), lambda i: (i,), memory_space=pltpu.SMEM),
                  pl.BlockSpec((tm, D), lambda i: (i, 0))],
        out_specs=pl.BlockSpec(memory_space=pl.ANY),
        out_shape=jax.ShapeDtypeStruct((n_rows, D), F32),
        scratch_shapes=[pltpu.SemaphoreType.DMA(())],
        compiler_params=pltpu.CompilerParams(
            dimension_semantics=("arbitrary",), has_side_effects=True),
        name="moe_dispatch",
    )(dest_flat, hn)


def _expert_kernel(vblk_ref, vexp_ref, vfirst_ref, gstart_ref, nvis_ref,
                   x_ref, wgu_ref, bgu_ref, wdn_ref, bdn_ref, y_ref):
    bm = x_ref.shape[0]
    ff = wdn_ref.shape[1]
    vis = pl.program_id(0)

    @pl.when(vis < nvis_ref[0])
    def _():
        e = vexp_ref[vis]
        row0 = vblk_ref[vis] * bm
        row = lax.broadcasted_iota(jnp.int32, (bm, 1), 0) + row0
        mine = (row >= gstart_ref[e]) & (row < gstart_ref[e + 1])
        xb = x_ref[...].astype(BF16)
        gu = jnp.dot(xb, wgu_ref[0], preferred_element_type=F32) + bgu_ref[0]
        g = jnp.minimum(gu[:, :ff], SWIGLU_LIMIT)
        up = jnp.clip(gu[:, ff:], -SWIGLU_LIMIT, SWIGLU_LIMIT)
        act = (up + 1.0) * (g * jax.nn.sigmoid(SWIGLU_ALPHA * g))
        y = jnp.dot(act.astype(BF16), wdn_ref[0], preferred_element_type=F32) + bdn_ref[0]

        @pl.when(vfirst_ref[vis] == 1)
        def _():
            y_ref[...] = jnp.where(mine, y, 0.0)

        @pl.when(vfirst_ref[vis] == 0)
        def _():
            y_ref[...] = jnp.where(mine, y, y_ref[...])


def _experts(xb, visits, w_gu, b_gu, w_dn, b_dn):
    M, D = xb.shape
    E, _, ff2 = w_gu.shape
    bm = EXPERT_BLOCK
    vblk, vexp, vfirst, gstart, nvis = visits
    row_map = lambda v, vb, ve, vf, gs, nv: (vb[v], 0)
    exp_map = lambda v, vb, ve, vf, gs, nv: (ve[v], 0, 0)
    grid_spec = pltpu.PrefetchScalarGridSpec(
        num_scalar_prefetch=5,
        grid=(vblk.shape[0],),
        in_specs=[
            pl.BlockSpec((bm, D), row_map),
            pl.BlockSpec((1, D, ff2), exp_map),
            pl.BlockSpec((1, 1, ff2), exp_map),
            pl.BlockSpec((1, ff2 // 2, D), exp_map),
            pl.BlockSpec((1, 1, D), exp_map),
        ],
        out_specs=pl.BlockSpec((bm, D), row_map),
    )
    return pl.pallas_call(
        _expert_kernel,
        grid_spec=grid_spec,
        out_shape=jax.ShapeDtypeStruct((M, D), F32),
        compiler_params=pltpu.CompilerParams(
            dimension_semantics=("arbitrary",), vmem_limit_bytes=VMEM_LIMIT),
        name="moe_experts",
    )(vblk, vexp, vfirst, gstart, nvis, xb, w_gu, b_gu.reshape(E, 1, ff2), w_dn,
      b_dn.reshape(E, 1, D))


def _visit_schedule(counts, n_rows, bm):
    n_blocks = n_rows // bm
    max_visits = n_blocks + N_EXPERTS - 1
    ends = jnp.cumsum(counts)
    starts = ends - counts
    first_blk = starts // bm
    last_blk = jnp.maximum(ends - 1, 0) // bm
    n_vis = jnp.where(counts > 0, last_blk - first_blk + 1, 0)
    vis_end = jnp.cumsum(n_vis)
    vis_start = vis_end - n_vis
    total = vis_end[-1]
    v = jnp.minimum(jnp.arange(max_visits, dtype=jnp.int32), total - 1)
    vexp = jnp.sum(v[:, None] >= vis_end[None, :], axis=-1).astype(jnp.int32)
    pick = lambda table: jnp.sum(
        jnp.where(vexp[:, None] == jnp.arange(N_EXPERTS, dtype=jnp.int32), table[None, :], 0), axis=-1)
    vblk = (pick(first_blk) + v - pick(vis_start)).astype(jnp.int32)
    vfirst = jnp.concatenate([jnp.ones((1,), jnp.int32),
                              (vblk[1:] != vblk[:-1]).astype(jnp.int32)])
    gstart = jnp.concatenate([starts, ends[-1:]]).astype(jnp.int32)
    return vblk, vexp, vfirst, gstart, total.astype(jnp.int32).reshape(1)


def _combine_kernel(dest_ref, dest_next_ref, yb_ref, gate_ref, h_ref, gf_ref, out_ref,
                    buf_ref, sem):
    tm = h_ref.shape[0]
    step = pl.program_id(0)
    last = pl.num_programs(0) - 1

    def issue(idx_ref, slot):
        for t in range(tm):
            for kk in range(TOP_K):
                pltpu.make_async_copy(
                    yb_ref.at[pl.ds(idx_ref[t * TOP_K + kk], 1)],
                    buf_ref.at[slot, kk, pl.ds(t, 1)], sem.at[slot]
                ).start(priority=kk % 2)

    def reduce(slot):
        for kk in range(TOP_K):
            pltpu.make_async_copy(yb_ref.at[pl.ds(0, tm)], buf_ref.at[slot, kk], sem.at[slot]).wait()
        gates = gate_ref[...]
        y = h_ref[...]
        for kk in range(TOP_K):
            y = y + gates[:, kk:kk + 1] * buf_ref[slot, kk]
        out_ref[...] = _rms(y, gf_ref[...])

    @pl.when(step == 0)
    def _():
        issue(dest_ref, 0)

    for slot in range(2):
        @pl.when(step % 2 == slot)
        def _():
            @pl.when(step < last)
            def _():
                issue(dest_next_ref, 1 - slot)
            reduce(slot)


def _combine(yb, dest_flat, gates, h, final_g):
    T, D = h.shape
    tm = COMBINE_TILE
    n_steps = T // tm
    return pl.pallas_call(
        _combine_kernel,
        grid=(n_steps,),
        in_specs=[pl.BlockSpec((tm * TOP_K,), lambda i: (i,), memory_space=pltpu.SMEM),
                  pl.BlockSpec((tm * TOP_K,), lambda i: (jnp.minimum(i + 1, n_steps - 1),),
                               memory_space=pltpu.SMEM),
                  pl.BlockSpec(memory_space=pl.ANY),
                  pl.BlockSpec((tm, TOP_K), lambda i: (i, 0)),
                  pl.BlockSpec((tm, D), lambda i: (i, 0)),
                  pl.BlockSpec((1, D), lambda i: (0, 0))],
        out_specs=pl.BlockSpec((tm, D), lambda i: (i, 0)),
        out_shape=jax.ShapeDtypeStruct((T, D), F32),
        scratch_shapes=[pltpu.VMEM((2, TOP_K, tm, D), F32), pltpu.SemaphoreType.DMA((2,))],
        compiler_params=pltpu.CompilerParams(
            dimension_semantics=("arbitrary",), vmem_limit_bytes=VMEM_LIMIT),
        name="moe_combine",
    )(dest_flat, dest_flat, yb, gates, h, final_g.reshape(1, D))


def _rope_tables(seq_len):
    half = HEAD_DIM // 2
    inv_freq = 1.0 / (ROPE_THETA ** (jnp.arange(0, HEAD_DIM, 2, dtype=F32) / HEAD_DIM))
    ang = jnp.arange(seq_len, dtype=F32)[:, None] * inv_freq[None, :]
    cos = jnp.tile(jnp.cos(ang), (1, LANES // half))
    sin = jnp.tile(jnp.concatenate([-jnp.sin(ang), jnp.sin(ang)], axis=-1), (1, LANES // HEAD_DIM))
    return cos, sin


def _moe(hn, h, idx, gates, rank, counts, w_gu, b_gu, w_dn, b_dn, final_g):
    T, D = hn.shape
    n_assign = T * TOP_K
    counts = counts.reshape(N_EXPERTS)
    starts = jnp.cumsum(counts) - counts
    onehot = idx[..., None] == jnp.arange(N_EXPERTS, dtype=jnp.int32)
    dest = rank + jnp.sum(jnp.where(onehot, starts, 0), axis=-1)
    dest_flat = dest.reshape(n_assign).astype(jnp.int32)
    visits = _visit_schedule(counts, n_assign, EXPERT_BLOCK)

    xb = _dispatch(hn, dest_flat, n_assign)
    yb = _experts(xb, visits, w_gu.astype(BF16), b_gu, w_dn.astype(BF16), b_dn)
    return _combine(yb, dest_flat, gates, h, final_g)


def kernel(x, norm1_g, w_mix_in, attn_norm_g, conv_w, conv_norm_g, w_mix_out, norm2_g,
           w_router, b_router, w_gate_up, b_gate_up, w_down, b_down, final_norm_g):
    B, S, D = x.shape
    assert norm1_g.shape[0] == 1, "single-layer block"
    cos_tab, sin_tab = _rope_tables(S)
    nat, d4, d16, conv_n = _in_projection(x, norm1_g[0], w_mix_in[0].astype(BF16), conv_w[0],
                                          conv_norm_g[0], cos_tab, sin_tab)
    attn = _dilated_attention(nat, d4, d16)

    T = B * S
    h, hn, idx, gates, rank, counts = _out_projection_router(
        attn.reshape(T, ATT_WIDTH), conv_n.reshape(T, CONV_WIDTH), x.reshape(T, D),
        attn_norm_g[0], w_mix_out[0].astype(BF16), norm2_g[0], w_router[0], b_router[0])
    out = _moe(hn, h, idx, gates, rank, counts, w_gate_up[0], b_gate_up[0], w_down[0],
               b_down[0], final_norm_g)
    return out.reshape(B, S, D)
```

```python
import functools

import jax
import jax.numpy as jnp
from jax import lax
from jax.experimental import pallas as pl
from jax.experimental.pallas import tpu as pltpu

F32 = jnp.float32
BF16 = jnp.bfloat16

N_HEADS = 8
HEAD_DIM = 64
ATT_WIDTH = N_HEADS * HEAD_DIM
CONV_WIDTH = 512
ATT_BLOCK = 128
ROPE_THETA = 10000.0
N_EXPERTS = 32
TOP_K = 4
SWIGLU_ALPHA = 1.702
SWIGLU_LIMIT = 7.0
NORM_EPS = 1e-5
NEG_BIG = -1e30

LANES = 128
N_PAIRS = ATT_WIDTH // LANES
VMEM_LIMIT = 52 * 1024 * 1024

PROJ_TILE = 512
ATT_ROWS = 512
EXPERT_BLOCK = 512
DISPATCH_TILE = 256
COMBINE_TILE = 128


def _rms(x, g):
    return x * lax.rsqrt(jnp.mean(x * x, axis=-1, keepdims=True) + NORM_EPS) * g


def _inproj_kernel(x_ref, g1_ref, w_ref, cos_ref, sin_ref, convw_ref, convg_ref,
                   nat_ref, d4_ref, d16_ref, c_ref, zbuf_ref, sa_ref, sb_ref):
    tm = x_ref.shape[1]
    hn = _rms(x_ref[0], g1_ref[...]).astype(BF16)
    proj = jnp.dot(hn, w_ref[...], preferred_element_type=F32)

    cos = cos_ref[...]
    sin = sin_ref[...]
    lane = lax.broadcasted_iota(jnp.int32, (tm, LANES), 1)
    first_half = (lane % HEAD_DIM) < (HEAD_DIM // 2)

    def rope(t):
        partner = jnp.where(first_half, pltpu.roll(t, LANES - HEAD_DIM // 2, 1),
                            pltpu.roll(t, HEAD_DIM // 2, 1))
        return t * cos + partner * sin

    for ti in range(3):
        for c in range(N_PAIRS):
            col = ti * ATT_WIDTH + c * LANES
            val = proj[:, col:col + LANES]
            if ti == 0:
                val = rope(val) * (HEAD_DIM ** -0.5)
            elif ti == 1:
                val = rope(val)
            nat_ref[ti, c] = val.astype(BF16)
            slab = ti * N_PAIRS + c
            sa_ref[slab] = val
            for r in range(4):
                piece = sa_ref[slab, pl.ds(r, tm // 4, stride=4), :]
                d4_ref[ti, c, r] = piece.astype(BF16)
                sb_ref[slab, r] = piece
            for r4 in range(4):
                for j in range(4):
                    piece = sb_ref[slab, r4, pl.ds(j, tm // 16, stride=4), :]
                    d16_ref[ti, c, r4 + 4 * j] = piece.astype(BF16)

    base = 3 * ATT_WIDTH
    u = proj[:, base:base + CONV_WIDTH]
    gate_b = proj[:, base + CONV_WIDTH:base + 2 * CONV_WIDTH]
    gate_c = proj[:, base + 2 * CONV_WIDTH:base + 3 * CONV_WIDTH]
    z = gate_c * u

    @pl.when(pl.program_id(1) == 0)
    def _():
        zbuf_ref[0:8, :] = jnp.zeros((8, CONV_WIDTH), F32)

    @pl.when(pl.program_id(1) != 0)
    def _():
        zbuf_ref[0:8, :] = zbuf_ref[tm:tm + 8, :]

    zbuf_ref[8:8 + tm, :] = z
    z1 = zbuf_ref[7:7 + tm, :]
    z2 = zbuf_ref[6:6 + tm, :]
    conv = z2 * convw_ref[0:1, :] + z1 * convw_ref[1:2, :] + z * convw_ref[2:3, :]
    c_ref[0] = _rms(gate_b * conv, convg_ref[...]).astype(BF16)


def _in_projection(x, norm1_g, w_in, conv_w, conv_g, cos_tab, sin_tab):
    B, S, D = x.shape
    tm = PROJ_TILE
    width = w_in.shape[1]
    full = lambda shape: pl.BlockSpec(shape, lambda b, i: (0,) * len(shape))
    nat = jax.ShapeDtypeStruct((3, N_PAIRS, B, S, LANES), BF16)
    d4 = jax.ShapeDtypeStruct((3, N_PAIRS, B, 4, S // 4, LANES), BF16)
    d16 = jax.ShapeDtypeStruct((3, N_PAIRS, B, 16, S // 16, LANES), BF16)
    return pl.pallas_call(
        _inproj_kernel,
        grid=(B, S // tm),
        in_specs=[
            pl.BlockSpec((1, tm, D), lambda b, i: (b, i, 0)),
            full((1, D)),
            full((D, width)),
            pl.BlockSpec((tm, LANES), lambda b, i: (i, 0)),
            pl.BlockSpec((tm, LANES), lambda b, i: (i, 0)),
            full((3, CONV_WIDTH)),
            full((1, CONV_WIDTH)),
        ],
        out_specs=[
            pl.BlockSpec((3, N_PAIRS, None, tm, LANES), lambda b, i: (0, 0, b, i, 0)),
            pl.BlockSpec((3, N_PAIRS, None, 4, tm // 4, LANES), lambda b, i: (0, 0, b, 0, i, 0)),
            pl.BlockSpec((3, N_PAIRS, None, 16, tm // 16, LANES), lambda b, i: (0, 0, b, 0, i, 0)),
            pl.BlockSpec((1, tm, CONV_WIDTH), lambda b, i: (b, i, 0)),
        ],
        out_shape=[nat, d4, d16, jax.ShapeDtypeStruct((B, S, CONV_WIDTH), BF16)],
        scratch_shapes=[pltpu.VMEM((tm + 8, CONV_WIDTH), F32),
                        pltpu.VMEM((3 * N_PAIRS, tm, LANES), F32),
                        pltpu.VMEM((3 * N_PAIRS, 4, tm // 4, LANES), F32)],
        compiler_params=pltpu.CompilerParams(
            dimension_semantics=("arbitrary", "arbitrary"), vmem_limit_bytes=VMEM_LIMIT),
        name="inproj",
    )(x, norm1_g.reshape(1, D), w_in, cos_tab, sin_tab, conv_w, conv_g.reshape(1, CONV_WIDTH))


def _band_bias(is_first):
    blk = ATT_BLOCK
    qi = lax.broadcasted_iota(jnp.int32, (blk, 2 * blk), 0)
    kj = lax.broadcasted_iota(jnp.int32, (blk, 2 * blk), 1)
    band = (kj >= qi) & (kj <= qi + blk)
    if is_first is not None:
        band = band & ((kj >= blk) | jnp.logical_not(is_first))
    return jnp.where(band, 0.0, NEG_BIG).astype(F32)


def _pair_attention(q2, k2, v2, bias, head0):
    o_pair = lse_pair = None
    for hh in range(2):
        sel = head0 if hh == 0 else jnp.logical_not(head0)
        qm = jnp.where(sel, q2, jnp.zeros_like(q2))
        s = lax.dot_general(qm, k2, (((1,), (1,)), ((), ())), preferred_element_type=F32) + bias
        m = jnp.max(s, axis=-1, keepdims=True)
        p = jnp.exp(s - m)
        l = jnp.sum(p, axis=-1, keepdims=True)
        o = jnp.dot(p.astype(BF16), v2, preferred_element_type=F32) / l
        lse = jnp.broadcast_to(m + jnp.log(l), o.shape)
        if hh == 0:
            o_pair, lse_pair = o, lse
        else:
            o_pair = jnp.where(head0, o_pair, o)
            lse_pair = jnp.where(head0, lse_pair, lse)
    return o_pair, lse_pair


def _merge(o_old, l_old, o_new, l_new):
    mx = jnp.maximum(l_old, l_new)
    wa = jnp.exp(l_old - mx)
    wb = jnp.exp(l_new - mx)
    return (wa * o_old + wb * o_new) / (wa + wb), mx + jnp.log(wa + wb)


def _attn_classes_kernel(*refs, has_prev):
    if has_prev:
        q_ref, kp_ref, kc_ref, vp_ref, vc_ref, op_ref, lp_ref, o_ref, lse_ref = refs
    else:
        q_ref, kp_ref, kc_ref, vp_ref, vc_ref, o_ref, lse_ref = refs
    blk = ATT_BLOCK
    bias = _band_bias(pl.program_id(1 if has_prev else 2) == 0)
    head0 = lax.broadcasted_iota(jnp.int32, (blk, LANES), 1) < HEAD_DIM
    for cls in range(4):
        for c in range(N_PAIRS):
            k2 = jnp.concatenate([kp_ref[c, cls], kc_ref[c, cls]], axis=0)
            v2 = jnp.concatenate([vp_ref[c, cls], vc_ref[c, cls]], axis=0)
            o_pair, lse_pair = _pair_attention(q_ref[c, cls], k2, v2, bias, head0)
            if has_prev:
                o_pair, lse_pair = _merge(op_ref[c, cls], lp_ref[c, cls], o_pair, lse_pair)
            o_ref[c, pl.ds(cls, blk, stride=4), :] = o_pair
            lse_ref[c, pl.ds(cls, blk, stride=4), :] = lse_pair


def _attn_natural_kernel(q_ref, kp_ref, kc_ref, vp_ref, vc_ref, op_ref, lp_ref, o_ref):
    blk = ATT_BLOCK
    rows_total = q_ref.shape[1]
    bias_band = _band_bias(None)
    bias_first = _band_bias(pl.program_id(1) == 0)
    head0 = lax.broadcasted_iota(jnp.int32, (blk, LANES), 1) < HEAD_DIM
    for j in range(rows_total // blk):
        rows = slice(j * blk, (j + 1) * blk)
        for c in range(N_PAIRS):
            if j == 0:
                k2 = jnp.concatenate([kp_ref[c], kc_ref[c, 0:blk]], axis=0)
                v2 = jnp.concatenate([vp_ref[c], vc_ref[c, 0:blk]], axis=0)
            else:
                k2 = kc_ref[c, (j - 1) * blk:(j + 1) * blk]
                v2 = vc_ref[c, (j - 1) * blk:(j + 1) * blk]
            o_pair, lse_pair = _pair_attention(q_ref[c, rows], k2, v2,
                                               bias_first if j == 0 else bias_band, head0)
            o_pair, _ = _merge(op_ref[c, rows], lp_ref[c, rows], o_pair, lse_pair)
            o_ref[rows, c * LANES:(c + 1) * LANES] = o_pair


def _dilated_attention(nat, d4, d16):
    _, _, B, S, _ = nat.shape
    blk = ATT_BLOCK
    cparams = lambda n: pltpu.CompilerParams(
        dimension_semantics=("arbitrary",) * n, vmem_limit_bytes=VMEM_LIMIT)

    L16, L4 = S // 16, S // 4
    d16v = d16.reshape(3, N_PAIRS, B, 4, 4, L16, LANES)
    def spec16(t, prev):
        row = (lambda n: jnp.maximum(n - 1, 0)) if prev else (lambda n: n)
        return pl.BlockSpec((None, N_PAIRS, None, 4, None, blk, LANES),
                            lambda b, r4, n: (t, 0, b, 0, r4, row(n), 0))
    state4 = jax.ShapeDtypeStruct((N_PAIRS, B, 4, L4, LANES), F32)
    out16 = pl.BlockSpec((N_PAIRS, None, None, 4 * blk, LANES), lambda b, r4, n: (0, b, r4, n, 0))
    o4, l4 = pl.pallas_call(
        functools.partial(_attn_classes_kernel, has_prev=False),
        grid=(B, 4, L16 // blk),
        in_specs=[spec16(0, False), spec16(1, True), spec16(1, False), spec16(2, True),
                  spec16(2, False)],
        out_specs=[out16, out16],
        out_shape=[state4, state4],
        compiler_params=cparams(3),
        name="dilated_attn_d16",
    )(d16v, d16v, d16v, d16v, d16v)

    def spec4(t, prev):
        row = (lambda n: jnp.maximum(n - 1, 0)) if prev else (lambda n: n)
        return pl.BlockSpec((None, N_PAIRS, None, 4, blk, LANES),
                            lambda b, n: (t, 0, b, 0, row(n), 0))
    in_state4 = pl.BlockSpec((N_PAIRS, None, 4, blk, LANES), lambda b, n: (0, b, 0, n, 0))
    state1 = jax.ShapeDtypeStruct((N_PAIRS, B, S, LANES), F32)
    out4 = pl.BlockSpec((N_PAIRS, None, 4 * blk, LANES), lambda b, n: (0, b, n, 0))
    o1, l1 = pl.pallas_call(
        functools.partial(_attn_classes_kernel, has_prev=True),
        grid=(B, L4 // blk),
        in_specs=[spec4(0, False), spec4(1, True), spec4(1, False), spec4(2, True),
                  spec4(2, False), in_state4, in_state4],
        out_specs=[out4, out4],
        out_shape=[state1, state1],
        compiler_params=cparams(2),
        name="dilated_attn_d4",
    )(d4, d4, d4, d4, d4, o4, l4)

    rows = ATT_ROWS
    per = rows // blk
    cur1 = lambda t: pl.BlockSpec((None, N_PAIRS, None, rows, LANES), lambda b, n: (t, 0, b, n, 0))
    prev1 = lambda t: pl.BlockSpec((None, N_PAIRS, None, blk, LANES),
                                   lambda b, n: (t, 0, b, jnp.maximum(n * per - 1, 0), 0))
    in_state1 = pl.BlockSpec((N_PAIRS, None, rows, LANES), lambda b, n: (0, b, n, 0))
    return pl.pallas_call(
        _attn_natural_kernel,
        grid=(B, S // rows),
        in_specs=[cur1(0), prev1(1), cur1(1), prev1(2), cur1(2), in_state1, in_state1],
        out_specs=pl.BlockSpec((None, rows, ATT_WIDTH), lambda b, n: (b, n, 0)),
        out_shape=jax.ShapeDtypeStruct((B, S, ATT_WIDTH), F32),
        compiler_params=cparams(2),
        name="dilated_attn_d1",
    )(nat, nat, nat, nat, nat, o1, l1)


def _split3(t):
    hi = t.astype(BF16)
    r1 = t - hi.astype(F32)
    mid = r1.astype(BF16)
    lo = (r1 - mid.astype(F32)).astype(BF16)
    return hi, mid, lo


def _outproj_router_kernel(attn_ref, conv_ref, x_ref, ga_ref, wo_ref, g2_ref, wrt_ref, br_ref,
                           h_ref, hn_ref, idx_ref, gate_ref, rank_ref, cnt_ref, carry_ref):
    tm = x_ref.shape[0]
    step = pl.program_id(0)

    @pl.when(step == 0)
    def _():
        carry_ref[...] = jnp.zeros_like(carry_ref)

    attn_n = _rms(attn_ref[...], ga_ref[...]).astype(BF16)
    mixed = jnp.dot(attn_n, wo_ref[0:ATT_WIDTH, :], preferred_element_type=F32)
    mixed += jnp.dot(conv_ref[...], wo_ref[ATT_WIDTH:, :], preferred_element_type=F32)
    h = x_ref[...] + mixed
    h_ref[...] = h
    hn = _rms(h, g2_ref[...])
    hn_ref[...] = hn

    a0, a1, a2 = _split3(hn)
    w0, w1, w2 = (wrt_ref[0], wrt_ref[1], wrt_ref[2])
    dot_t = lambda w, a: lax.dot_general(w, a, (((1,), (1,)), ((), ())),
                                         preferred_element_type=F32)
    logits = (dot_t(w0, a0) + (dot_t(w1, a0) + dot_t(w0, a1))
              + (dot_t(w2, a0) + dot_t(w1, a1) + dot_t(w0, a2))) + br_ref[...]

    eidx = lax.broadcasted_iota(jnp.int32, (N_EXPERTS, tm), 0)
    work = logits
    vals, idxs = [], []
    for _ in range(TOP_K):
        mval = jnp.max(work, axis=0, keepdims=True)
        midx = jnp.min(jnp.where(work == mval, eidx, N_EXPERTS), axis=0, keepdims=True)
        vals.append(mval)
        idxs.append(midx)
        work = jnp.where(eidx == midx, -jnp.inf, work)
    exps = [jnp.exp(v - vals[0]) for v in vals]
    denom = exps[0] + exps[1] + exps[2] + exps[3]

    chosen = jnp.zeros((N_EXPERTS, tm), F32)
    for midx in idxs:
        chosen = chosen + (eidx == midx).astype(F32)
    si = lax.broadcasted_iota(jnp.int32, (tm, tm), 0)
    ti = lax.broadcasted_iota(jnp.int32, (tm, tm), 1)
    earlier = (si < ti).astype(BF16)
    before = jnp.dot(chosen.astype(BF16), earlier, preferred_element_type=F32) + carry_ref[...]
    for kk in range(TOP_K):
        idx_ref[kk:kk + 1, :] = idxs[kk]
        gate_ref[kk:kk + 1, :] = exps[kk] / denom
        rank_ref[kk:kk + 1, :] = jnp.sum(
            jnp.where(eidx == idxs[kk], before, 0.0), axis=0, keepdims=True).astype(jnp.int32)
    carry_ref[...] = carry_ref[...] + jnp.sum(chosen, axis=1, keepdims=True)
    cnt_ref[...] = carry_ref[...].astype(jnp.int32)


def _out_projection_router(attn, conv_n, x, attn_g, w_out, norm2_g, w_router, b_router):
    T, D = x.shape
    tm = PROJ_TILE
    tok = lambda width: pl.BlockSpec((tm, width), lambda i: (i, 0))
    per_k = pl.BlockSpec((TOP_K, tm), lambda i: (0, i))
    full = lambda shape: pl.BlockSpec(shape, lambda i: (0,) * len(shape))
    wrt_pieces = jnp.stack(_split3(w_router.T))
    return pl.pallas_call(
        _outproj_router_kernel,
        grid=(T // tm,),
        in_specs=[tok(ATT_WIDTH), tok(CONV_WIDTH), tok(D), full((1, ATT_WIDTH)),
                  full((ATT_WIDTH + CONV_WIDTH, D)), full((1, D)),
                  full((3, N_EXPERTS, D)), full((N_EXPERTS, 1))],
        out_specs=[tok(D), tok(D), per_k, per_k, per_k, full((N_EXPERTS, 1))],
        out_shape=[jax.ShapeDtypeStruct((T, D), F32), jax.ShapeDtypeStruct((T, D), F32),
                   jax.ShapeDtypeStruct((TOP_K, T), jnp.int32),
                   jax.ShapeDtypeStruct((TOP_K, T), F32),
                   jax.ShapeDtypeStruct((TOP_K, T), jnp.int32),
                   jax.ShapeDtypeStruct((N_EXPERTS, 1), jnp.int32)],
        scratch_shapes=[pltpu.VMEM((N_EXPERTS, 1), F32)],
        compiler_params=pltpu.CompilerParams(
            dimension_semantics=("arbitrary",), vmem_limit_bytes=VMEM_LIMIT),
        name="outproj_router",
    )(attn, conv_n, x, attn_g.reshape(1, -1), w_out, norm2_g.reshape(1, D),
      wrt_pieces, b_router.reshape(N_EXPERTS, 1))


def _dispatch_kernel(dest_ref, hn_ref, xb_ref, sem):
    tm = hn_ref.shape[0]

    for t in range(tm):
        for kk in range(TOP_K):
            pltpu.make_async_copy(
                hn_ref.at[pl.ds(t, 1)], xb_ref.at[pl.ds(dest_ref[t * TOP_K + kk], 1)], sem
            ).start(priority=kk % 2)
    for _ in range(TOP_K):
        pltpu.make_async_copy(hn_ref, xb_ref.at[pl.ds(0, tm)], sem).wait()


def _dispatch(hn, dest_flat, n_rows):
    T, D = hn.shape
    tm = DISPATCH_TILE
    return pl.pallas_call(
        _dispatch_kernel,
        grid=(T // tm,),
        in_specs=[pl.BlockSpec((tm * TOP_K,), lambda i: (i,), memory_space=pltpu.SMEM),
                  pl.BlockSpec((tm, D), lambda i: (i, 0))],
        out_specs=pl.BlockSpec(memory_space=pl.ANY),
        out_shape=jax.ShapeDtypeStruct((n_rows, D), F32),
        scratch_shapes=[pltpu.SemaphoreType.DMA(())],
        compiler_params=pltpu.CompilerParams(
            dimension_semantics=("arbitrary",), has_side_effects=True),
        name="moe_dispatch",
    )(dest_flat, hn)


def _expert_kernel(vblk_ref, vexp_ref, vfirst_ref, vnew_ref, gstart_ref, nvis_ref,
                   x_ref, wgu_ref, bgu_ref, wdn_ref, bdn_ref, y_ref, wgu_bf_ref, wdn_bf_ref):
    bm = x_ref.shape[0]
    ff = wdn_ref.shape[1]
    vis = pl.program_id(0)

    @pl.when(vis < nvis_ref[0])
    def _():
        @pl.when(vnew_ref[vis] == 1)
        def _():
            wgu_bf_ref[...] = wgu_ref[0].astype(BF16)
            wdn_bf_ref[...] = wdn_ref[0].astype(BF16)

        e = vexp_ref[vis]
        row0 = vblk_ref[vis] * bm
        row = lax.broadcasted_iota(jnp.int32, (bm, 1), 0) + row0
        mine = (row >= gstart_ref[e]) & (row < gstart_ref[e + 1])
        xb = x_ref[...].astype(BF16)
        gu = jnp.dot(xb, wgu_bf_ref[...], preferred_element_type=F32) + bgu_ref[0]
        g = jnp.minimum(gu[:, :ff], SWIGLU_LIMIT)
        up = jnp.clip(gu[:, ff:], -SWIGLU_LIMIT, SWIGLU_LIMIT)
        act = (up + 1.0) * (g * jax.nn.sigmoid(SWIGLU_ALPHA * g))
        y = jnp.dot(act.astype(BF16), wdn_bf_ref[...], preferred_element_type=F32) + bdn_ref[0]

        @pl.when(vfirst_ref[vis] == 1)
        def _():
            y_ref[...] = jnp.where(mine, y, 0.0)

        @pl.when(vfirst_ref[vis] == 0)
        def _():
            y_ref[...] = jnp.where(mine, y, y_ref[...])


def _experts(xb, visits, w_gu, b_gu, w_dn, b_dn):
    M, D = xb.shape
    E, _, ff2 = w_gu.shape
    bm = EXPERT_BLOCK
    vblk, vexp, vfirst, vnew, gstart, nvis = visits
    row_map = lambda v, vb, ve, vf, vn, gs, nv: (vb[v], 0)
    exp_map = lambda v, vb, ve, vf, vn, gs, nv: (ve[v], 0, 0)
    grid_spec = pltpu.PrefetchScalarGridSpec(
        num_scalar_prefetch=6,
        grid=(vblk.shape[0],),
        in_specs=[
            pl.BlockSpec((bm, D), row_map),
            pl.BlockSpec((1, D, ff2), exp_map),
            pl.BlockSpec((1, 1, ff2), exp_map),
            pl.BlockSpec((1, ff2 // 2, D), exp_map),
            pl.BlockSpec((1, 1, D), exp_map),
        ],
        out_specs=pl.BlockSpec((bm, D), row_map),
        scratch_shapes=[pltpu.VMEM((D, ff2), BF16), pltpu.VMEM((ff2 // 2, D), BF16)],
    )
    return pl.pallas_call(
        _expert_kernel,
        grid_spec=grid_spec,
        out_shape=jax.ShapeDtypeStruct((M, D), F32),
        compiler_params=pltpu.CompilerParams(
            dimension_semantics=("arbitrary",), vmem_limit_bytes=VMEM_LIMIT),
        name="moe_experts",
    )(vblk, vexp, vfirst, vnew, gstart, nvis, xb, w_gu, b_gu.reshape(E, 1, ff2), w_dn,
      b_dn.reshape(E, 1, D))


def _visit_schedule(counts, n_rows, bm):
    n_blocks = n_rows // bm
    max_visits = n_blocks + N_EXPERTS - 1
    ends = jnp.cumsum(counts)
    starts = ends - counts
    first_blk = starts // bm
    last_blk = jnp.maximum(ends - 1, 0) // bm
    n_vis = jnp.where(counts > 0, last_blk - first_blk + 1, 0)
    vis_end = jnp.cumsum(n_vis)
    vis_start = vis_end - n_vis
    total = vis_end[-1]
    v = jnp.minimum(jnp.arange(max_visits, dtype=jnp.int32), total - 1)
    vexp = jnp.sum(v[:, None] >= vis_end[None, :], axis=-1).astype(jnp.int32)
    pick = lambda table: jnp.sum(
        jnp.where(vexp[:, None] == jnp.arange(N_EXPERTS, dtype=jnp.int32), table[None, :], 0), axis=-1)
    vblk = (pick(first_blk) + v - pick(vis_start)).astype(jnp.int32)
    vfirst = jnp.concatenate([jnp.ones((1,), jnp.int32),
                              (vblk[1:] != vblk[:-1]).astype(jnp.int32)])
    vnew = jnp.concatenate([jnp.ones((1,), jnp.int32),
                            (vexp[1:] != vexp[:-1]).astype(jnp.int32)])
    gstart = jnp.concatenate([starts, ends[-1:]]).astype(jnp.int32)
    return vblk, vexp, vfirst, vnew, gstart, total.astype(jnp.int32).reshape(1)


def _combine_kernel(dest_ref, dest_next_ref, yb_ref, gate_ref, h_ref, gf_ref, out_ref,
                    buf_ref, sem):
    tm = h_ref.shape[0]
    step = pl.program_id(0)
    last = pl.num_programs(0) - 1

    def issue(idx_ref, slot):
        for t in range(tm):
            for kk in range(TOP_K):
                pltpu.make_async_copy(
                    yb_ref.at[pl.ds(idx_ref[t * TOP_K + kk], 1)],
                    buf_ref.at[slot, kk, pl.ds(t, 1)], sem.at[slot]
                ).start(priority=kk % 2)

    def reduce(slot):
        for kk in range(TOP_K):
            pltpu.make_async_copy(yb_ref.at[pl.ds(0, tm)], buf_ref.at[slot, kk], sem.at[slot]).wait()
        gates = gate_ref[...]
        y = h_ref[...]
        for kk in range(TOP_K):
            y = y + gates[:, kk:kk + 1] * buf_ref[slot, kk]
        out_ref[...] = _rms(y, gf_ref[...])

    @pl.when(step == 0)
    def _():
        issue(dest_ref, 0)

    for slot in range(2):
        @pl.when(step % 2 == slot)
        def _():
            @pl.when(step < last)
            def _():
                issue(dest_next_ref, 1 - slot)
            reduce(slot)


def _combine(yb, dest_flat, gates, h, final_g):
    T, D = h.shape
    tm = COMBINE_TILE
    n_steps = T // tm
    return pl.pallas_call(
        _combine_kernel,
        grid=(n_steps,),
        in_specs=[pl.BlockSpec((tm * TOP_K,), lambda i: (i,), memory_space=pltpu.SMEM),
                  pl.BlockSpec((tm * TOP_K,), lambda i: (jnp.minimum(i + 1, n_steps - 1),),
                               memory_space=pltpu.SMEM),
                  pl.BlockSpec(memory_space=pl.ANY),
                  pl.BlockSpec((tm, TOP_K), lambda i: (i, 0)),
                  pl.BlockSpec((tm, D), lambda i: (i, 0)),
                  pl.BlockSpec((1, D), lambda i: (0, 0))],
        out_specs=pl.BlockSpec((tm, D), lambda i: (i, 0)),
        out_shape=jax.ShapeDtypeStruct((T, D), F32),
        scratch_shapes=[pltpu.VMEM((2, TOP_K, tm, D), F32), pltpu.SemaphoreType.DMA((2,))],
        compiler_params=pltpu.CompilerParams(
            dimension_semantics=("arbitrary",), vmem_limit_bytes=VMEM_LIMIT),
        name="moe_combine",
    )(dest_flat, dest_flat, yb, gates, h, final_g.reshape(1, D))


def _rope_tables(seq_len):
    half = HEAD_DIM // 2
    inv_freq = 1.0 / (ROPE_THETA ** (jnp.arange(0, HEAD_DIM, 2, dtype=F32) / HEAD_DIM))
    ang = jnp.arange(seq_len, dtype=F32)[:, None] * inv_freq[None, :]
    cos = jnp.tile(jnp.cos(ang), (1, LANES // half))
    sin = jnp.tile(jnp.concatenate([-jnp.sin(ang), jnp.sin(ang)], axis=-1), (1, LANES // HEAD_DIM))
    return cos, sin


def _moe(hn, h, idx, gates, rank, counts, w_gu, b_gu, w_dn, b_dn, final_g):
    T, D = hn.shape
    n_assign = T * TOP_K
    counts = counts.reshape(N_EXPERTS)
    starts = jnp.cumsum(counts) - counts
    onehot = idx[..., None] == jnp.arange(N_EXPERTS, dtype=jnp.int32)
    dest = rank + jnp.sum(jnp.where(onehot, starts, 0), axis=-1)
    dest_flat = dest.T.reshape(n_assign).astype(jnp.int32)
    visits = _visit_schedule(counts, n_assign, EXPERT_BLOCK)

    xb = _dispatch(hn, dest_flat, n_assign)
    yb = _experts(xb, visits, w_gu, b_gu, w_dn, b_dn)
    return _combine(yb, dest_flat, gates.T, h, final_g)


def kernel(x, norm1_g, w_mix_in, attn_norm_g, conv_w, conv_norm_g, w_mix_out, norm2_g,
           w_router, b_router, w_gate_up, b_gate_up, w_down, b_down, final_norm_g):
    B, S, D = x.shape
    assert norm1_g.shape[0] == 1, "single-layer block"
    cos_tab, sin_tab = _rope_tables(S)
    nat, d4, d16, conv_n = _in_projection(x, norm1_g[0], w_mix_in[0].astype(BF16), conv_w[0],
                                          conv_norm_g[0], cos_tab, sin_tab)
    attn = _dilated_attention(nat, d4, d16)

    T = B * S
    h, hn, idx, gates, rank, counts = _out_projection_router(
        attn.reshape(T, ATT_WIDTH), conv_n.reshape(T, CONV_WIDTH), x.reshape(T, D),
        attn_norm_g[0], w_mix_out[0].astype(BF16), norm2_g[0], w_router[0], b_router[0])
    out = _moe(hn, h, idx, gates, rank, counts, w_gate_up[0], b_gate_up[0], w_down[0],
               b_down[0], final_norm_g)
    return out.reshape(B, S, D)
```

```python
import functools

import jax
import jax.numpy as jnp
from jax import lax
from jax.experimental import pallas as pl
from jax.experimental.pallas import tpu as pltpu

F32 = jnp.float32
BF16 = jnp.bfloat16

N_HEADS = 8
HEAD_DIM = 64
ATT_WIDTH = N_HEADS * HEAD_DIM
CONV_WIDTH = 512
ATT_BLOCK = 128
ROPE_THETA = 10000.0
N_EXPERTS = 32
TOP_K = 4
SWIGLU_ALPHA = 1.702
SWIGLU_LIMIT = 7.0
NORM_EPS = 1e-5
NEG_BIG = -1e30

LANES = 128
N_PAIRS = ATT_WIDTH // LANES
VMEM_LIMIT = 52 * 1024 * 1024

PROJ_TILE = 512
ATT_ROWS = 512
EXPERT_BLOCK = 512
DISPATCH_TILE = 512
COMBINE_TILE = 256


def _rms(x, g):
    return x * lax.rsqrt(jnp.mean(x * x, axis=-1, keepdims=True) + NORM_EPS) * g


def _inproj_kernel(x_ref, g1_ref, w_ref, cos_ref, sin_ref, convw_ref, convg_ref,
                   nat_ref, d4_ref, d16_ref, c_ref, zbuf_ref, sa_ref, sb_ref):
    tm = x_ref.shape[1]
    hn = _rms(x_ref[0], g1_ref[...]).astype(BF16)
    proj = jnp.dot(hn, w_ref[...], preferred_element_type=F32)

    cos = cos_ref[...]
    sin = sin_ref[...]
    lane = lax.broadcasted_iota(jnp.int32, (tm, LANES), 1)
    first_half = (lane % HEAD_DIM) < (HEAD_DIM // 2)

    def rope(t):
        partner = jnp.where(first_half, pltpu.roll(t, LANES - HEAD_DIM // 2, 1),
                            pltpu.roll(t, HEAD_DIM // 2, 1))
        return t * cos + partner * sin

    for ti in range(3):
        for c in range(N_PAIRS):
            col = ti * ATT_WIDTH + c * LANES
            val = proj[:, col:col + LANES]
            if ti == 0:
                val = rope(val) * (HEAD_DIM ** -0.5)
            elif ti == 1:
                val = rope(val)
            nat_ref[ti, c] = val.astype(BF16)
            slab = ti * N_PAIRS + c
            sa_ref[slab] = val
            for r in range(4):
                piece = sa_ref[slab, pl.ds(r, tm // 4, stride=4), :]
                d4_ref[ti, c, r] = piece.astype(BF16)
                sb_ref[slab, r] = piece
            for r4 in range(4):
                for j in range(4):
                    piece = sb_ref[slab, r4, pl.ds(j, tm // 16, stride=4), :]
                    d16_ref[ti, c, r4 + 4 * j] = piece.astype(BF16)

    base = 3 * ATT_WIDTH
    u = proj[:, base:base + CONV_WIDTH]
    gate_b = proj[:, base + CONV_WIDTH:base + 2 * CONV_WIDTH]
    gate_c = proj[:, base + 2 * CONV_WIDTH:base + 3 * CONV_WIDTH]
    z = gate_c * u

    @pl.when(pl.program_id(1) == 0)
    def _():
        zbuf_ref[0:8, :] = jnp.zeros((8, CONV_WIDTH), F32)

    @pl.when(pl.program_id(1) != 0)
    def _():
        zbuf_ref[0:8, :] = zbuf_ref[tm:tm + 8, :]

    zbuf_ref[8:8 + tm, :] = z
    z1 = zbuf_ref[7:7 + tm, :]
    z2 = zbuf_ref[6:6 + tm, :]
    conv = z2 * convw_ref[0:1, :] + z1 * convw_ref[1:2, :] + z * convw_ref[2:3, :]
    c_ref[0] = _rms(gate_b * conv, convg_ref[...]).astype(BF16)


def _in_projection(x, norm1_g, w_in, conv_w, conv_g, cos_tab, sin_tab):
    B, S, D = x.shape
    tm = PROJ_TILE
    width = w_in.shape[1]
    full = lambda shape: pl.BlockSpec(shape, lambda b, i: (0,) * len(shape))
    nat = jax.ShapeDtypeStruct((3, N_PAIRS, B, S, LANES), BF16)
    d4 = jax.ShapeDtypeStruct((3, N_PAIRS, B, 4, S // 4, LANES), BF16)
    d16 = jax.ShapeDtypeStruct((3, N_PAIRS, B, 16, S // 16, LANES), BF16)
    return pl.pallas_call(
        _inproj_kernel,
        grid=(B, S // tm),
        in_specs=[
            pl.BlockSpec((1, tm, D), lambda b, i: (b, i, 0)),
            full((1, D)),
            full((D, width)),
            pl.BlockSpec((tm, LANES), lambda b, i: (i, 0)),
            pl.BlockSpec((tm, LANES), lambda b, i: (i, 0)),
            full((3, CONV_WIDTH)),
            full((1, CONV_WIDTH)),
        ],
        out_specs=[
            pl.BlockSpec((3, N_PAIRS, None, tm, LANES), lambda b, i: (0, 0, b, i, 0)),
            pl.BlockSpec((3, N_PAIRS, None, 4, tm // 4, LANES), lambda b, i: (0, 0, b, 0, i, 0)),
            pl.BlockSpec((3, N_PAIRS, None, 16, tm // 16, LANES), lambda b, i: (0, 0, b, 0, i, 0)),
            pl.BlockSpec((1, tm, CONV_WIDTH), lambda b, i: (b, i, 0)),
        ],
        out_shape=[nat, d4, d16, jax.ShapeDtypeStruct((B, S, CONV_WIDTH), BF16)],
        scratch_shapes=[pltpu.VMEM((tm + 8, CONV_WIDTH), F32),
                        pltpu.VMEM((3 * N_PAIRS, tm, LANES), F32),
                        pltpu.VMEM((3 * N_PAIRS, 4, tm // 4, LANES), F32)],
        compiler_params=pltpu.CompilerParams(
            dimension_semantics=("arbitrary", "arbitrary"), vmem_limit_bytes=VMEM_LIMIT),
        name="inproj",
    )(x, norm1_g.reshape(1, D), w_in, cos_tab, sin_tab, conv_w, conv_g.reshape(1, CONV_WIDTH))


def _band_bias(is_first):
    blk = ATT_BLOCK
    qi = lax.broadcasted_iota(jnp.int32, (blk, 2 * blk), 0)
    kj = lax.broadcasted_iota(jnp.int32, (blk, 2 * blk), 1)
    band = (kj >= qi) & (kj <= qi + blk)
    if is_first is not None:
        band = band & ((kj >= blk) | jnp.logical_not(is_first))
    return jnp.where(band, 0.0, NEG_BIG).astype(F32)


def _pair_attention(q2, k2, v2, bias, head0):
    parts = []
    for hh in range(2):
        sel = head0 if hh == 0 else jnp.logical_not(head0)
        qm = jnp.where(sel, q2, jnp.zeros_like(q2))
        s = lax.dot_general(qm, k2, (((1,), (1,)), ((), ())), preferred_element_type=F32) + bias
        m = jnp.max(s, axis=-1, keepdims=True)
        p = jnp.exp(s - m)
        l = jnp.sum(p, axis=-1, keepdims=True)
        parts.append((m, l, jnp.dot(p.astype(BF16), v2, preferred_element_type=F32)))
    (m0, l0, o0), (m1, l1, o1) = parts
    shape = o0.shape
    pick = lambda a, b: jnp.where(head0, jnp.broadcast_to(a, shape), jnp.broadcast_to(b, shape))
    return pick(m0, m1), pick(l0, l1), jnp.where(head0, o0, o1)


def _finish(m, l, o_un):
    return o_un * (1.0 / l), m + jnp.log(l)


def _merge(o_old, lse_old, m, l, o_un, want_lse=True):
    mx = jnp.maximum(lse_old, m)
    wa = jnp.exp(lse_old - mx)
    eb = jnp.exp(m - mx)
    tot = wa + eb * l
    o = (wa * o_old + eb * o_un) * (1.0 / tot)
    return o, (mx + jnp.log(tot) if want_lse else None)


def _attn_classes_kernel(*refs, has_prev):
    if has_prev:
        q_ref, kp_ref, kc_ref, vp_ref, vc_ref, op_ref, lp_ref, o_ref, lse_ref = refs
    else:
        q_ref, kp_ref, kc_ref, vp_ref, vc_ref, o_ref, lse_ref = refs
    blk = ATT_BLOCK
    bias = _band_bias(pl.program_id(1 if has_prev else 2) == 0)
    head0 = lax.broadcasted_iota(jnp.int32, (blk, LANES), 1) < HEAD_DIM
    for cls in range(4):
        for c in range(N_PAIRS):
            k2 = jnp.concatenate([kp_ref[c, cls], kc_ref[c, cls]], axis=0)
            v2 = jnp.concatenate([vp_ref[c, cls], vc_ref[c, cls]], axis=0)
            m, l, o_un = _pair_attention(q_ref[c, cls], k2, v2, bias, head0)
            if has_prev:
                o_pair, lse_pair = _merge(op_ref[c, cls], lp_ref[c, cls], m, l, o_un)
            else:
                o_pair, lse_pair = _finish(m, l, o_un)
            o_ref[c, pl.ds(cls, blk, stride=4), :] = o_pair
            lse_ref[c, pl.ds(cls, blk, stride=4), :] = lse_pair


def _attn_natural_kernel(q_ref, kp_ref, kc_ref, vp_ref, vc_ref, op_ref, lp_ref, o_ref):
    blk = ATT_BLOCK
    rows_total = q_ref.shape[1]
    bias_band = _band_bias(None)
    bias_first = _band_bias(pl.program_id(1) == 0)
    head0 = lax.broadcasted_iota(jnp.int32, (blk, LANES), 1) < HEAD_DIM
    for j in range(rows_total // blk):
        rows = slice(j * blk, (j + 1) * blk)
        for c in range(N_PAIRS):
            if j == 0:
                k2 = jnp.concatenate([kp_ref[c], kc_ref[c, 0:blk]], axis=0)
                v2 = jnp.concatenate([vp_ref[c], vc_ref[c, 0:blk]], axis=0)
            else:
                k2 = kc_ref[c, (j - 1) * blk:(j + 1) * blk]
                v2 = vc_ref[c, (j - 1) * blk:(j + 1) * blk]
            m, l, o_un = _pair_attention(q_ref[c, rows], k2, v2,
                                         bias_first if j == 0 else bias_band, head0)
            o_pair, _ = _merge(op_ref[c, rows], lp_ref[c, rows], m, l, o_un, want_lse=False)
            o_ref[rows, c * LANES:(c + 1) * LANES] = o_pair


def _dilated_attention(nat, d4, d16):
    _, _, B, S, _ = nat.shape
    blk = ATT_BLOCK
    cparams = lambda n: pltpu.CompilerParams(
        dimension_semantics=("arbitrary",) * n, vmem_limit_bytes=VMEM_LIMIT)

    L16, L4 = S // 16, S // 4
    d16v = d16.reshape(3, N_PAIRS, B, 4, 4, L16, LANES)
    def spec16(t, prev):
        row = (lambda n: jnp.maximum(n - 1, 0)) if prev else (lambda n: n)
        return pl.BlockSpec((None, N_PAIRS, None, 4, None, blk, LANES),
                            lambda b, r4, n: (t, 0, b, 0, r4, row(n), 0))
    state4 = jax.ShapeDtypeStruct((N_PAIRS, B, 4, L4, LANES), F32)
    out16 = pl.BlockSpec((N_PAIRS, None, None, 4 * blk, LANES), lambda b, r4, n: (0, b, r4, n, 0))
    o4, l4 = pl.pallas_call(
        functools.partial(_attn_classes_kernel, has_prev=False),
        grid=(B, 4, L16 // blk),
        in_specs=[spec16(0, False), spec16(1, True), spec16(1, False), spec16(2, True),
                  spec16(2, False)],
        out_specs=[out16, out16],
        out_shape=[state4, state4],
        compiler_params=cparams(3),
        name="dilated_attn_d16",
    )(d16v, d16v, d16v, d16v, d16v)

    def spec4(t, prev):
        row = (lambda n: jnp.maximum(n - 1, 0)) if prev else (lambda n: n)
        return pl.BlockSpec((None, N_PAIRS, None, 4, blk, LANES),
                            lambda b, n: (t, 0, b, 0, row(n), 0))
    in_state4 = pl.BlockSpec((N_PAIRS, None, 4, blk, LANES), lambda b, n: (0, b, 0, n, 0))
    state1 = jax.ShapeDtypeStruct((N_PAIRS, B, S, LANES), F32)
    out4 = pl.BlockSpec((N_PAIRS, None, 4 * blk, LANES), lambda b, n: (0, b, n, 0))
    o1, l1 = pl.pallas_call(
        functools.partial(_attn_classes_kernel, has_prev=True),
        grid=(B, L4 // blk),
        in_specs=[spec4(0, False), spec4(1, True), spec4(1, False), spec4(2, True),
                  spec4(2, False), in_state4, in_state4],
        out_specs=[out4, out4],
        out_shape=[state1, state1],
        compiler_params=cparams(2),
        name="dilated_attn_d4",
    )(d4, d4, d4, d4, d4, o4, l4)

    rows = ATT_ROWS
    per = rows // blk
    cur1 = lambda t: pl.BlockSpec((None, N_PAIRS, None, rows, LANES), lambda b, n: (t, 0, b, n, 0))
    prev1 = lambda t: pl.BlockSpec((None, N_PAIRS, None, blk, LANES),
                                   lambda b, n: (t, 0, b, jnp.maximum(n * per - 1, 0), 0))
    in_state1 = pl.BlockSpec((N_PAIRS, None, rows, LANES), lambda b, n: (0, b, n, 0))
    return pl.pallas_call(
        _attn_natural_kernel,
        grid=(B, S // rows),
        in_specs=[cur1(0), prev1(1), cur1(1), prev1(2), cur1(2), in_state1, in_state1],
        out_specs=pl.BlockSpec((None, rows, ATT_WIDTH), lambda b, n: (b, n, 0)),
        out_shape=jax.ShapeDtypeStruct((B, S, ATT_WIDTH), F32),
        compiler_params=cparams(2),
        name="dilated_attn_d1",
    )(nat, nat, nat, nat, nat, o1, l1)


def _split3(t):
    hi = t.astype(BF16)
    r1 = t - hi.astype(F32)
    mid = r1.astype(BF16)
    lo = (r1 - mid.astype(F32)).astype(BF16)
    return hi, mid, lo


def _outproj_router_kernel(attn_ref, conv_ref, x_ref, ga_ref, wo_ref, g2_ref, wrt_ref, br_ref,
                           h_ref, hn_ref, idx_ref, gate_ref, rank_ref, cnt_ref, carry_ref):
    tm = x_ref.shape[0]
    step = pl.program_id(0)

    @pl.when(step == 0)
    def _():
        carry_ref[...] = jnp.zeros_like(carry_ref)

    attn_n = _rms(attn_ref[...], ga_ref[...]).astype(BF16)
    mixed = jnp.dot(attn_n, wo_ref[0:ATT_WIDTH, :], preferred_element_type=F32)
    mixed += jnp.dot(conv_ref[...], wo_ref[ATT_WIDTH:, :], preferred_element_type=F32)
    h = x_ref[...] + mixed
    h_ref[...] = h
    hn = _rms(h, g2_ref[...])
    hn_ref[...] = hn

    a0, a1, a2 = _split3(hn)
    w0, w1, w2 = (wrt_ref[0], wrt_ref[1], wrt_ref[2])
    dot_t = lambda w, a: lax.dot_general(w, a, (((1,), (1,)), ((), ())),
                                         preferred_element_type=F32)
    logits = (dot_t(w0, a0) + (dot_t(w1, a0) + dot_t(w0, a1))
              + (dot_t(w2, a0) + dot_t(w1, a1) + dot_t(w0, a2))) + br_ref[...]

    eidx = lax.broadcasted_iota(jnp.int32, (N_EXPERTS, tm), 0)
    work = logits
    vals, idxs = [], []
    for _ in range(TOP_K):
        mval = jnp.max(work, axis=0, keepdims=True)
        midx = jnp.min(jnp.where(work == mval, eidx, N_EXPERTS), axis=0, keepdims=True)
        vals.append(mval)
        idxs.append(midx)
        work = jnp.where(eidx == midx, -jnp.inf, work)
    exps = [jnp.exp(v - vals[0]) for v in vals]
    denom = exps[0] + exps[1] + exps[2] + exps[3]

    chosen = jnp.zeros((N_EXPERTS, tm), F32)
    for midx in idxs:
        chosen = chosen + (eidx == midx).astype(F32)
    si = lax.broadcasted_iota(jnp.int32, (tm, tm), 0)
    ti = lax.broadcasted_iota(jnp.int32, (tm, tm), 1)
    earlier = (si < ti).astype(BF16)
    before = jnp.dot(chosen.astype(BF16), earlier, preferred_element_type=F32) + carry_ref[...]
    for kk in range(TOP_K):
        idx_ref[kk:kk + 1, :] = idxs[kk]
        gate_ref[kk:kk + 1, :] = exps[kk] / denom
        rank_ref[kk:kk + 1, :] = jnp.sum(
            jnp.where(eidx == idxs[kk], before, 0.0), axis=0, keepdims=True).astype(jnp.int32)
    carry_ref[...] = carry_ref[...] + jnp.sum(chosen, axis=1, keepdims=True)
    cnt_ref[...] = carry_ref[...].astype(jnp.int32)


def _out_projection_router(attn, conv_n, x, attn_g, w_out, norm2_g, w_router, b_router):
    T, D = x.shape
    tm = PROJ_TILE
    tok = lambda width: pl.BlockSpec((tm, width), lambda i: (i, 0))
    per_k = pl.BlockSpec((TOP_K, tm), lambda i: (0, i))
    full = lambda shape: pl.BlockSpec(shape, lambda i: (0,) * len(shape))
    wrt_pieces = jnp.stack(_split3(w_router.T))
    return pl.pallas_call(
        _outproj_router_kernel,
        grid=(T // tm,),
        in_specs=[tok(ATT_WIDTH), tok(CONV_WIDTH), tok(D), full((1, ATT_WIDTH)),
                  full((ATT_WIDTH + CONV_WIDTH, D)), full((1, D)),
                  full((3, N_EXPERTS, D)), full((N_EXPERTS, 1))],
        out_specs=[tok(D), tok(D), per_k, per_k, per_k, full((N_EXPERTS, 1))],
        out_shape=[jax.ShapeDtypeStruct((T, D), F32), jax.ShapeDtypeStruct((T, D), F32),
                   jax.ShapeDtypeStruct((TOP_K, T), jnp.int32),
                   jax.ShapeDtypeStruct((TOP_K, T), F32),
                   jax.ShapeDtypeStruct((TOP_K, T), jnp.int32),
                   jax.ShapeDtypeStruct((N_EXPERTS, 1), jnp.int32)],
        scratch_shapes=[pltpu.VMEM((N_EXPERTS, 1), F32)],
        compiler_params=pltpu.CompilerParams(
            dimension_semantics=("arbitrary",), vmem_limit_bytes=VMEM_LIMIT),
        name="outproj_router",
    )(attn, conv_n, x, attn_g.reshape(1, -1), w_out, norm2_g.reshape(1, D),
      wrt_pieces, b_router.reshape(N_EXPERTS, 1))


def _dispatch_kernel(dest_ref, hn_ref, xb_ref, sem):
    tm = hn_ref.shape[0]

    for t in range(tm):
        for kk in range(TOP_K):
            pltpu.make_async_copy(
                hn_ref.at[pl.ds(t, 1)], xb_ref.at[pl.ds(dest_ref[t * TOP_K + kk], 1)], sem
            ).start(priority=kk % 2)
    for _ in range(TOP_K):
        pltpu.make_async_copy(hn_ref, xb_ref.at[pl.ds(0, tm)], sem).wait()


def _dispatch(hn, dest_flat, n_rows):
    T, D = hn.shape
    tm = DISPATCH_TILE
    return pl.pallas_call(
        _dispatch_kernel,
        grid=(T // tm,),
        in_specs=[pl.BlockSpec((tm * TOP_K,), lambda i: (i,), memory_space=pltpu.SMEM),
                  pl.BlockSpec((tm, D), lambda i: (i, 0))],
        out_specs=pl.BlockSpec(memory_space=pl.ANY),
        out_shape=jax.ShapeDtypeStruct((n_rows, D), F32),
        scratch_shapes=[pltpu.SemaphoreType.DMA(())],
        compiler_params=pltpu.CompilerParams(
            dimension_semantics=("arbitrary",), has_side_effects=True),
        name="moe_dispatch",
    )(dest_flat, hn)


def _expert_kernel(vblk_ref, vexp_ref, vfirst_ref, vnew_ref, gstart_ref, nvis_ref,
                   x_ref, wgu_ref, bgu_ref, wdn_ref, bdn_ref, y_ref, wgu_bf_ref, wdn_bf_ref):
    bm = x_ref.shape[0]
    ff = wdn_ref.shape[1]
    vis = pl.program_id(0)

    @pl.when(vis < nvis_ref[0])
    def _():
        @pl.when(vnew_ref[vis] == 1)
        def _():
            wgu_bf_ref[...] = wgu_ref[0].astype(BF16)
            wdn_bf_ref[...] = wdn_ref[0].astype(BF16)

        e = vexp_ref[vis]
        row0 = vblk_ref[vis] * bm
        row = lax.broadcasted_iota(jnp.int32, (bm, 1), 0) + row0
        mine = (row >= gstart_ref[e]) & (row < gstart_ref[e + 1])
        xb = x_ref[...].astype(BF16)
        gu = jnp.dot(xb, wgu_bf_ref[...], preferred_element_type=F32) + bgu_ref[0]
        g = jnp.minimum(gu[:, :ff], SWIGLU_LIMIT)
        up = jnp.clip(gu[:, ff:], -SWIGLU_LIMIT, SWIGLU_LIMIT)
        act = (up + 1.0) * (g * jax.nn.sigmoid(SWIGLU_ALPHA * g))
        y = jnp.dot(act.astype(BF16), wdn_bf_ref[...], preferred_element_type=F32) + bdn_ref[0]

        @pl.when(vfirst_ref[vis] == 1)
        def _():
            y_ref[...] = jnp.where(mine, y, 0.0)

        @pl.when(vfirst_ref[vis] == 0)
        def _():
            y_ref[...] = jnp.where(mine, y, y_ref[...])


def _experts(xb, visits, w_gu, b_gu, w_dn, b_dn):
    M, D = xb.shape
    E, _, ff2 = w_gu.shape
    bm = EXPERT_BLOCK
    vblk, vexp, vfirst, vnew, gstart, nvis = visits
    row_map = lambda v, vb, ve, vf, vn, gs, nv: (vb[v], 0)
    exp_map = lambda v, vb, ve, vf, vn, gs, nv: (ve[v], 0, 0)
    grid_spec = pltpu.PrefetchScalarGridSpec(
        num_scalar_prefetch=6,
        grid=(vblk.shape[0],),
        in_specs=[
            pl.BlockSpec((bm, D), row_map),
            pl.BlockSpec((1, D, ff2), exp_map),
            pl.BlockSpec((1, 1, ff2), exp_map),
            pl.BlockSpec((1, ff2 // 2, D), exp_map),
            pl.BlockSpec((1, 1, D), exp_map),
        ],
        out_specs=pl.BlockSpec((bm, D), row_map),
        scratch_shapes=[pltpu.VMEM((D, ff2), BF16), pltpu.VMEM((ff2 // 2, D), BF16)],
    )
    return pl.pallas_call(
        _expert_kernel,
        grid_spec=grid_spec,
        out_shape=jax.ShapeDtypeStruct((M, D), F32),
        compiler_params=pltpu.CompilerParams(
            dimension_semantics=("arbitrary",), vmem_limit_bytes=VMEM_LIMIT),
        name="moe_experts",
    )(vblk, vexp, vfirst, vnew, gstart, nvis, xb, w_gu, b_gu.reshape(E, 1, ff2), w_dn,
      b_dn.reshape(E, 1, D))


def _visit_schedule(counts, n_rows, bm):
    n_blocks = n_rows // bm
    max_visits = n_blocks + N_EXPERTS - 1
    ends = jnp.cumsum(counts)
    starts = ends - counts
    first_blk = starts // bm
    last_blk = jnp.maximum(ends - 1, 0) // bm
    n_vis = jnp.where(counts > 0, last_blk - first_blk + 1, 0)
    vis_end = jnp.cumsum(n_vis)
    vis_start = vis_end - n_vis
    total = vis_end[-1]
    v = jnp.minimum(jnp.arange(max_visits, dtype=jnp.int32), total - 1)
    vexp = jnp.sum(v[:, None] >= vis_end[None, :], axis=-1).astype(jnp.int32)
    pick = lambda table: jnp.sum(
        jnp.where(vexp[:, None] == jnp.arange(N_EXPERTS, dtype=jnp.int32), table[None, :], 0), axis=-1)
    vblk = (pick(first_blk) + v - pick(vis_start)).astype(jnp.int32)
    vfirst = jnp.concatenate([jnp.ones((1,), jnp.int32),
                              (vblk[1:] != vblk[:-1]).astype(jnp.int32)])
    vnew = jnp.concatenate([jnp.ones((1,), jnp.int32),
                            (vexp[1:] != vexp[:-1]).astype(jnp.int32)])
    gstart = jnp.concatenate([starts, ends[-1:]]).astype(jnp.int32)
    return vblk, vexp, vfirst, vnew, gstart, total.astype(jnp.int32).reshape(1)


def _combine_kernel(dest_ref, dest_next_ref, yb_ref, gate_ref, h_ref, gf_ref, out_ref,
                    buf_ref, sem):
    tm = h_ref.shape[0]
    step = pl.program_id(0)
    last = pl.num_programs(0) - 1

    def issue(idx_ref, slot):
        for t in range(tm):
            for kk in range(TOP_K):
                pltpu.make_async_copy(
                    yb_ref.at[pl.ds(idx_ref[t * TOP_K + kk], 1)],
                    buf_ref.at[slot, kk, pl.ds(t, 1)], sem.at[slot]
                ).start(priority=kk % 2)

    def reduce(slot):
        for kk in range(TOP_K):
            pltpu.make_async_copy(yb_ref.at[pl.ds(0, tm)], buf_ref.at[slot, kk], sem.at[slot]).wait()
        gates = gate_ref[...]
        y = h_ref[...]
        for kk in range(TOP_K):
            y = y + gates[:, kk:kk + 1] * buf_ref[slot, kk]
        out_ref[...] = _rms(y, gf_ref[...])

    @pl.when(step == 0)
    def _():
        issue(dest_ref, 0)

    for slot in range(2):
        @pl.when(step % 2 == slot)
        def _():
            @pl.when(step < last)
            def _():
                issue(dest_next_ref, 1 - slot)
            reduce(slot)


def _combine(yb, dest_flat, gates, h, final_g):
    T, D = h.shape
    tm = COMBINE_TILE
    n_steps = T // tm
    return pl.pallas_call(
        _combine_kernel,
        grid=(n_steps,),
        in_specs=[pl.BlockSpec((tm * TOP_K,), lambda i: (i,), memory_space=pltpu.SMEM),
                  pl.BlockSpec((tm * TOP_K,), lambda i: (jnp.minimum(i + 1, n_steps - 1),),
                               memory_space=pltpu.SMEM),
                  pl.BlockSpec(memory_space=pl.ANY),
                  pl.BlockSpec((tm, TOP_K), lambda i: (i, 0)),
                  pl.BlockSpec((tm, D), lambda i: (i, 0)),
                  pl.BlockSpec((1, D), lambda i: (0, 0))],
        out_specs=pl.BlockSpec((tm, D), lambda i: (i, 0)),
        out_shape=jax.ShapeDtypeStruct((T, D), F32),
        scratch_shapes=[pltpu.VMEM((2, TOP_K, tm, D), F32), pltpu.SemaphoreType.DMA((2,))],
        compiler_params=pltpu.CompilerParams(
            dimension_semantics=("arbitrary",), vmem_limit_bytes=VMEM_LIMIT),
        name="moe_combine",
    )(dest_flat, dest_flat, yb, gates, h, final_g.reshape(1, D))


def _rope_tables(seq_len):
    half = HEAD_DIM // 2
    inv_freq = 1.0 / (ROPE_THETA ** (jnp.arange(0, HEAD_DIM, 2, dtype=F32) / HEAD_DIM))
    ang = jnp.arange(seq_len, dtype=F32)[:, None] * inv_freq[None, :]
    cos = jnp.tile(jnp.cos(ang), (1, LANES // half))
    sin = jnp.tile(jnp.concatenate([-jnp.sin(ang), jnp.sin(ang)], axis=-1), (1, LANES // HEAD_DIM))
    return cos, sin


def _moe(hn, h, idx, gates, rank, counts, w_gu, b_gu, w_dn, b_dn, final_g):
    T, D = hn.shape
    n_assign = T * TOP_K
    counts = counts.reshape(N_EXPERTS)
    starts = jnp.cumsum(counts) - counts
    onehot = idx[..., None] == jnp.arange(N_EXPERTS, dtype=jnp.int32)
    dest = rank + jnp.sum(jnp.where(onehot, starts, 0), axis=-1)
    dest_flat = dest.T.reshape(n_assign).astype(jnp.int32)
    visits = _visit_schedule(counts, n_assign, EXPERT_BLOCK)

    xb = _dispatch(hn, dest_flat, n_assign)
    yb = _experts(xb, visits, w_gu, b_gu, w_dn, b_dn)
    return _combine(yb, dest_flat, gates.T, h, final_g)


def kernel(x, norm1_g, w_mix_in, attn_norm_g, conv_w, conv_norm_g, w_mix_out, norm2_g,
           w_router, b_router, w_gate_up, b_gate_up, w_down, b_down, final_norm_g):
    B, S, D = x.shape
    assert norm1_g.shape[0] == 1, "single-layer block"
    cos_tab, sin_tab = _rope_tables(S)
    nat, d4, d16, conv_n = _in_projection(x, norm1_g[0], w_mix_in[0].astype(BF16), conv_w[0],
                                          conv_norm_g[0], cos_tab, sin_tab)
    attn = _dilated_attention(nat, d4, d16)

    T = B * S
    h, hn, idx, gates, rank, counts = _out_projection_router(
        attn.reshape(T, ATT_WIDTH), conv_n.reshape(T, CONV_WIDTH), x.reshape(T, D),
        attn_norm_g[0], w_mix_out[0].astype(BF16), norm2_g[0], w_router[0], b_router[0])
    out = _moe(hn, h, idx, gates, rank, counts, w_gate_up[0], b_gate_up[0], w_down[0],
               b_down[0], final_norm_g)
    return out.reshape(B, S, D)
```

```python
import functools

import jax
import jax.numpy as jnp
from jax import lax
from jax.experimental import pallas as pl
from jax.experimental.pallas import tpu as pltpu

F32 = jnp.float32
BF16 = jnp.bfloat16

N_HEADS = 8
HEAD_DIM = 64
ATT_WIDTH = N_HEADS * HEAD_DIM
CONV_WIDTH = 512
ATT_BLOCK = 128
ROPE_THETA = 10000.0
N_EXPERTS = 32
TOP_K = 4
SWIGLU_ALPHA = 1.702
SWIGLU_LIMIT = 7.0
NORM_EPS = 1e-5
NEG_BIG = -1e30

LANES = 128
N_PAIRS = ATT_WIDTH // LANES
VMEM_LIMIT = 52 * 1024 * 1024

PROJ_TILE = 512
ATT_ROWS = 1024
ATT_CLASS_ROWS = 256
EXPERT_BLOCK = 512
DISPATCH_TILE = 512
COMBINE_TILE = 512


def _rms(x, g):
    return x * lax.rsqrt(jnp.mean(x * x, axis=-1, keepdims=True) + NORM_EPS) * g


def _inproj_kernel(x_ref, g1_ref, w_ref, cos_ref, sin_ref, convw_ref, convg_ref,
                   nat_ref, d4_ref, d16_ref, c_ref, zbuf_ref, sa_ref, sb_ref):
    tm = x_ref.shape[1]
    hn = _rms(x_ref[0], g1_ref[...]).astype(BF16)
    proj = jnp.dot(hn, w_ref[...], preferred_element_type=F32)

    cos = cos_ref[...]
    sin = sin_ref[...]
    lane = lax.broadcasted_iota(jnp.int32, (tm, LANES), 1)
    first_half = (lane % HEAD_DIM) < (HEAD_DIM // 2)

    def rope(t):
        partner = jnp.where(first_half, pltpu.roll(t, LANES - HEAD_DIM // 2, 1),
                            pltpu.roll(t, HEAD_DIM // 2, 1))
        return t * cos + partner * sin

    for ti in range(3):
        for c in range(N_PAIRS):
            col = ti * ATT_WIDTH + c * LANES
            val = proj[:, col:col + LANES]
            if ti == 0:
                val = rope(val) * (HEAD_DIM ** -0.5)
            elif ti == 1:
                val = rope(val)
            nat_ref[ti, c] = val.astype(BF16)
            slab = ti * N_PAIRS + c
            sa_ref[slab] = val
            for r in range(4):
                piece = sa_ref[slab, pl.ds(r, tm // 4, stride=4), :]
                d4_ref[ti, c, r] = piece.astype(BF16)
                sb_ref[slab, r] = piece
            for r4 in range(4):
                for j in range(4):
                    piece = sb_ref[slab, r4, pl.ds(j, tm // 16, stride=4), :]
                    d16_ref[ti, c, r4 + 4 * j] = piece.astype(BF16)

    base = 3 * ATT_WIDTH
    u = proj[:, base:base + CONV_WIDTH]
    gate_b = proj[:, base + CONV_WIDTH:base + 2 * CONV_WIDTH]
    gate_c = proj[:, base + 2 * CONV_WIDTH:base + 3 * CONV_WIDTH]
    z = gate_c * u

    @pl.when(pl.program_id(1) == 0)
    def _():
        zbuf_ref[0:8, :] = jnp.zeros((8, CONV_WIDTH), F32)

    @pl.when(pl.program_id(1) != 0)
    def _():
        zbuf_ref[0:8, :] = zbuf_ref[tm:tm + 8, :]

    zbuf_ref[8:8 + tm, :] = z
    z1 = zbuf_ref[7:7 + tm, :]
    z2 = zbuf_ref[6:6 + tm, :]
    conv = z2 * convw_ref[0:1, :] + z1 * convw_ref[1:2, :] + z * convw_ref[2:3, :]
    c_ref[0] = _rms(gate_b * conv, convg_ref[...]).astype(BF16)


def _in_projection(x, norm1_g, w_in, conv_w, conv_g, cos_tab, sin_tab):
    B, S, D = x.shape
    tm = PROJ_TILE
    width = w_in.shape[1]
    full = lambda shape: pl.BlockSpec(shape, lambda b, i: (0,) * len(shape))
    nat = jax.ShapeDtypeStruct((3, N_PAIRS, B, S, LANES), BF16)
    d4 = jax.ShapeDtypeStruct((3, N_PAIRS, B, 4, S // 4, LANES), BF16)
    d16 = jax.ShapeDtypeStruct((3, N_PAIRS, B, 16, S // 16, LANES), BF16)
    return pl.pallas_call(
        _inproj_kernel,
        grid=(B, S // tm),
        in_specs=[
            pl.BlockSpec((1, tm, D), lambda b, i: (b, i, 0)),
            full((1, D)),
            full((D, width)),
            pl.BlockSpec((tm, LANES), lambda b, i: (i, 0)),
            pl.BlockSpec((tm, LANES), lambda b, i: (i, 0)),
            full((3, CONV_WIDTH)),
            full((1, CONV_WIDTH)),
        ],
        out_specs=[
            pl.BlockSpec((3, N_PAIRS, None, tm, LANES), lambda b, i: (0, 0, b, i, 0)),
            pl.BlockSpec((3, N_PAIRS, None, 4, tm // 4, LANES), lambda b, i: (0, 0, b, 0, i, 0)),
            pl.BlockSpec((3, N_PAIRS, None, 16, tm // 16, LANES), lambda b, i: (0, 0, b, 0, i, 0)),
            pl.BlockSpec((1, tm, CONV_WIDTH), lambda b, i: (b, i, 0)),
        ],
        out_shape=[nat, d4, d16, jax.ShapeDtypeStruct((B, S, CONV_WIDTH), BF16)],
        scratch_shapes=[pltpu.VMEM((tm + 8, CONV_WIDTH), F32),
                        pltpu.VMEM((3 * N_PAIRS, tm, LANES), F32),
                        pltpu.VMEM((3 * N_PAIRS, 4, tm // 4, LANES), F32)],
        compiler_params=pltpu.CompilerParams(
            dimension_semantics=("arbitrary", "arbitrary"), vmem_limit_bytes=VMEM_LIMIT),
        name="inproj",
    )(x, norm1_g.reshape(1, D), w_in, cos_tab, sin_tab, conv_w, conv_g.reshape(1, CONV_WIDTH))


def _band_bias(is_first):
    blk = ATT_BLOCK
    qi = lax.broadcasted_iota(jnp.int32, (blk, 2 * blk), 0)
    kj = lax.broadcasted_iota(jnp.int32, (blk, 2 * blk), 1)
    band = (kj >= qi) & (kj <= qi + blk)
    if is_first is not None:
        band = band & ((kj >= blk) | jnp.logical_not(is_first))
    return jnp.where(band, 0.0, NEG_BIG).astype(F32)


def _pair_attention(q2, k2, v2, bias, head0):
    parts = []
    for hh in range(2):
        sel = head0 if hh == 0 else jnp.logical_not(head0)
        qm = jnp.where(sel, q2, jnp.zeros_like(q2))
        s = lax.dot_general(qm, k2, (((1,), (1,)), ((), ())), preferred_element_type=F32) + bias
        m = jnp.max(s, axis=-1, keepdims=True)
        p = jnp.exp(s - m)
        l = jnp.sum(p, axis=-1, keepdims=True)
        parts.append((m, l, jnp.dot(p.astype(BF16), v2, preferred_element_type=F32)))
    (m0, l0, o0), (m1, l1, o1) = parts
    shape = o0.shape
    pick = lambda a, b: jnp.where(head0, jnp.broadcast_to(a, shape), jnp.broadcast_to(b, shape))
    return pick(m0, m1), pick(l0, l1), jnp.where(head0, o0, o1)


def _finish(m, l, o_un):
    return o_un * (1.0 / l), m + jnp.log(l)


def _merge(o_old, lse_old, m, l, o_un, want_lse=True):
    mx = jnp.maximum(lse_old, m)
    wa = jnp.exp(lse_old - mx)
    eb = jnp.exp(m - mx)
    tot = wa + eb * l
    o = (wa * o_old + eb * o_un) * (1.0 / tot)
    return o, (mx + jnp.log(tot) if want_lse else None)


def _attn_classes_kernel(*refs, has_prev):
    if has_prev:
        q_ref, kp_ref, kc_ref, vp_ref, vc_ref, op_ref, lp_ref, o_ref, lse_ref = refs
    else:
        q_ref, kp_ref, kc_ref, vp_ref, vc_ref, o_ref, lse_ref = refs
    blk = ATT_BLOCK
    rows_total = q_ref.shape[2]
    bias_band = _band_bias(None)
    bias_first = _band_bias(pl.program_id(1 if has_prev else 2) == 0)
    head0 = lax.broadcasted_iota(jnp.int32, (blk, LANES), 1) < HEAD_DIM
    for cls in range(4):
        for j in range(rows_total // blk):
            rows = slice(j * blk, (j + 1) * blk)
            for c in range(N_PAIRS):
                if j == 0:
                    k2 = jnp.concatenate([kp_ref[c, cls], kc_ref[c, cls, 0:blk]], axis=0)
                    v2 = jnp.concatenate([vp_ref[c, cls], vc_ref[c, cls, 0:blk]], axis=0)
                else:
                    k2 = kc_ref[c, cls, (j - 1) * blk:(j + 1) * blk]
                    v2 = vc_ref[c, cls, (j - 1) * blk:(j + 1) * blk]
                m, l, o_un = _pair_attention(q_ref[c, cls, rows], k2, v2,
                                             bias_first if j == 0 else bias_band, head0)
                if has_prev:
                    o_pair, lse_pair = _merge(op_ref[c, cls, rows], lp_ref[c, cls, rows], m, l, o_un)
                else:
                    o_pair, lse_pair = _finish(m, l, o_un)
                out_rows = pl.ds(cls + 4 * blk * j, blk, stride=4)
                o_ref[c, out_rows, :] = o_pair
                lse_ref[c, out_rows, :] = lse_pair


def _attn_natural_kernel(q_ref, kp_ref, kc_ref, vp_ref, vc_ref, op_ref, lp_ref, o_ref):
    blk = ATT_BLOCK
    rows_total = q_ref.shape[1]
    bias_band = _band_bias(None)
    bias_first = _band_bias(pl.program_id(1) == 0)
    head0 = lax.broadcasted_iota(jnp.int32, (blk, LANES), 1) < HEAD_DIM
    for j in range(rows_total // blk):
        rows = slice(j * blk, (j + 1) * blk)
        for c in range(N_PAIRS):
            if j == 0:
                k2 = jnp.concatenate([kp_ref[c], kc_ref[c, 0:blk]], axis=0)
                v2 = jnp.concatenate([vp_ref[c], vc_ref[c, 0:blk]], axis=0)
            else:
                k2 = kc_ref[c, (j - 1) * blk:(j + 1) * blk]
                v2 = vc_ref[c, (j - 1) * blk:(j + 1) * blk]
            m, l, o_un = _pair_attention(q_ref[c, rows], k2, v2,
                                         bias_first if j == 0 else bias_band, head0)
            o_pair, _ = _merge(op_ref[c, rows], lp_ref[c, rows], m, l, o_un, want_lse=False)
            o_ref[rows, c * LANES:(c + 1) * LANES] = o_pair


def _dilated_attention(nat, d4, d16):
    _, _, B, S, _ = nat.shape
    blk = ATT_BLOCK
    cparams = lambda n: pltpu.CompilerParams(
        dimension_semantics=("arbitrary",) * n, vmem_limit_bytes=VMEM_LIMIT)

    L16, L4 = S // 16, S // 4
    d16v = d16.reshape(3, N_PAIRS, B, 4, 4, L16, LANES)
    crows = ATT_CLASS_ROWS
    cper = crows // blk
    def spec16(t, prev):
        if prev:
            return pl.BlockSpec((None, N_PAIRS, None, 4, None, blk, LANES),
                                lambda b, r4, n: (t, 0, b, 0, r4, jnp.maximum(n * cper - 1, 0), 0))
        return pl.BlockSpec((None, N_PAIRS, None, 4, None, crows, LANES),
                            lambda b, r4, n: (t, 0, b, 0, r4, n, 0))
    state4 = jax.ShapeDtypeStruct((N_PAIRS, B, 4, L4, LANES), F32)
    out16 = pl.BlockSpec((N_PAIRS, None, None, 4 * crows, LANES), lambda b, r4, n: (0, b, r4, n, 0))
    o4, l4 = pl.pallas_call(
        functools.partial(_attn_classes_kernel, has_prev=False),
        grid=(B, 4, L16 // crows),
        in_specs=[spec16(0, False), spec16(1, True), spec16(1, False), spec16(2, True),
                  spec16(2, False)],
        out_specs=[out16, out16],
        out_shape=[state4, state4],
        compiler_params=cparams(3),
        name="dilated_attn_d16",
    )(d16v, d16v, d16v, d16v, d16v)

    def spec4(t, prev):
        if prev:
            return pl.BlockSpec((None, N_PAIRS, None, 4, blk, LANES),
                                lambda b, n: (t, 0, b, 0, jnp.maximum(n * cper - 1, 0), 0))
        return pl.BlockSpec((None, N_PAIRS, None, 4, crows, LANES), lambda b, n: (t, 0, b, 0, n, 0))
    in_state4 = pl.BlockSpec((N_PAIRS, None, 4, crows, LANES), lambda b, n: (0, b, 0, n, 0))
    state1 = jax.ShapeDtypeStruct((N_PAIRS, B, S, LANES), F32)
    out4 = pl.BlockSpec((N_PAIRS, None, 4 * crows, LANES), lambda b, n: (0, b, n, 0))
    o1, l1 = pl.pallas_call(
        functools.partial(_attn_classes_kernel, has_prev=True),
        grid=(B, L4 // crows),
        in_specs=[spec4(0, False), spec4(1, True), spec4(1, False), spec4(2, True),
                  spec4(2, False), in_state4, in_state4],
        out_specs=[out4, out4],
        out_shape=[state1, state1],
        compiler_params=cparams(2),
        name="dilated_attn_d4",
    )(d4, d4, d4, d4, d4, o4, l4)

    rows = ATT_ROWS
    per = rows // blk
    cur1 = lambda t: pl.BlockSpec((None, N_PAIRS, None, rows, LANES), lambda b, n: (t, 0, b, n, 0))
    prev1 = lambda t: pl.BlockSpec((None, N_PAIRS, None, blk, LANES),
                                   lambda b, n: (t, 0, b, jnp.maximum(n * per - 1, 0), 0))
    in_state1 = pl.BlockSpec((N_PAIRS, None, rows, LANES), lambda b, n: (0, b, n, 0))
    return pl.pallas_call(
        _attn_natural_kernel,
        grid=(B, S // rows),
        in_specs=[cur1(0), prev1(1), cur1(1), prev1(2), cur1(2), in_state1, in_state1],
        out_specs=pl.BlockSpec((None, rows, ATT_WIDTH), lambda b, n: (b, n, 0)),
        out_shape=jax.ShapeDtypeStruct((B, S, ATT_WIDTH), F32),
        compiler_params=cparams(2),
        name="dilated_attn_d1",
    )(nat, nat, nat, nat, nat, o1, l1)


def _split3(t):
    hi = t.astype(BF16)
    r1 = t - hi.astype(F32)
    mid = r1.astype(BF16)
    lo = (r1 - mid.astype(F32)).astype(BF16)
    return hi, mid, lo


def _outproj_router_kernel(attn_ref, conv_ref, x_ref, ga_ref, wo_ref, g2_ref, wrt_ref, br_ref,
                           h_ref, hn_ref, idx_ref, gate_ref, rank_ref, cnt_ref, carry_ref):
    tm = x_ref.shape[0]
    step = pl.program_id(0)

    @pl.when(step == 0)
    def _():
        carry_ref[...] = jnp.zeros_like(carry_ref)

    attn_n = _rms(attn_ref[...], ga_ref[...]).astype(BF16)
    mixed = jnp.dot(attn_n, wo_ref[0:ATT_WIDTH, :], preferred_element_type=F32)
    mixed += jnp.dot(conv_ref[...], wo_ref[ATT_WIDTH:, :], preferred_element_type=F32)
    h = x_ref[...] + mixed
    h_ref[...] = h
    hn = _rms(h, g2_ref[...])
    hn_ref[...] = hn

    a0, a1, a2 = _split3(hn)
    dot_t = lambda w, a: lax.dot_general(w, a, (((1,), (1,)), ((), ())),
                                         preferred_element_type=F32)
    ne = N_EXPERTS
    p0 = dot_t(wrt_ref[...], a0)
    p1 = dot_t(wrt_ref[0:2 * ne, :], a1)
    p2 = dot_t(wrt_ref[0:ne, :], a2)
    logits = (p0[0:ne] + (p0[ne:2 * ne] + p1[0:ne])
              + (p0[2 * ne:] + p1[ne:] + p2)) + br_ref[...]

    eidx = lax.broadcasted_iota(jnp.int32, (N_EXPERTS, tm), 0)
    work = logits
    vals, idxs = [], []
    for _ in range(TOP_K):
        mval = jnp.max(work, axis=0, keepdims=True)
        midx = jnp.min(jnp.where(work == mval, eidx, N_EXPERTS), axis=0, keepdims=True)
        vals.append(mval)
        idxs.append(midx)
        work = jnp.where(eidx == midx, -jnp.inf, work)
    exps = [jnp.exp(v - vals[0]) for v in vals]
    denom = exps[0] + exps[1] + exps[2] + exps[3]

    chosen = jnp.zeros((N_EXPERTS, tm), F32)
    for midx in idxs:
        chosen = chosen + (eidx == midx).astype(F32)
    si = lax.broadcasted_iota(jnp.int32, (tm, tm), 0)
    ti = lax.broadcasted_iota(jnp.int32, (tm, tm), 1)
    earlier = (si < ti).astype(BF16)
    before = jnp.dot(chosen.astype(BF16), earlier, preferred_element_type=F32) + carry_ref[...]
    for kk in range(TOP_K):
        idx_ref[kk:kk + 1, :] = idxs[kk]
        gate_ref[kk:kk + 1, :] = exps[kk] / denom
        rank_ref[kk:kk + 1, :] = jnp.sum(
            jnp.where(eidx == idxs[kk], before, 0.0), axis=0, keepdims=True).astype(jnp.int32)
    carry_ref[...] = carry_ref[...] + jnp.sum(chosen, axis=1, keepdims=True)
    cnt_ref[...] = carry_ref[...].astype(jnp.int32)


def _out_projection_router(attn, conv_n, x, attn_g, w_out, norm2_g, w_router, b_router):
    T, D = x.shape
    tm = PROJ_TILE
    tok = lambda width: pl.BlockSpec((tm, width), lambda i: (i, 0))
    per_k = pl.BlockSpec((TOP_K, tm), lambda i: (0, i))
    full = lambda shape: pl.BlockSpec(shape, lambda i: (0,) * len(shape))
    wrt_pieces = jnp.concatenate(_split3(w_router.T), axis=0)
    return pl.pallas_call(
        _outproj_router_kernel,
        grid=(T // tm,),
        in_specs=[tok(ATT_WIDTH), tok(CONV_WIDTH), tok(D), full((1, ATT_WIDTH)),
                  full((ATT_WIDTH + CONV_WIDTH, D)), full((1, D)),
                  full((3 * N_EXPERTS, D)), full((N_EXPERTS, 1))],
        out_specs=[tok(D), tok(D), per_k, per_k, per_k, full((N_EXPERTS, 1))],
        out_shape=[jax.ShapeDtypeStruct((T, D), F32), jax.ShapeDtypeStruct((T, D), F32),
                   jax.ShapeDtypeStruct((TOP_K, T), jnp.int32),
                   jax.ShapeDtypeStruct((TOP_K, T), F32),
                   jax.ShapeDtypeStruct((TOP_K, T), jnp.int32),
                   jax.ShapeDtypeStruct((N_EXPERTS, 1), jnp.int32)],
        scratch_shapes=[pltpu.VMEM((N_EXPERTS, 1), F32)],
        compiler_params=pltpu.CompilerParams(
            dimension_semantics=("arbitrary",), vmem_limit_bytes=VMEM_LIMIT),
        name="outproj_router",
    )(attn, conv_n, x, attn_g.reshape(1, -1), w_out, norm2_g.reshape(1, D),
      wrt_pieces, b_router.reshape(N_EXPERTS, 1))


def _dispatch_kernel(dest_ref, hn_ref, xb_ref, sem):
    tm = hn_ref.shape[0]

    for t in range(tm):
        for kk in range(TOP_K):
            pltpu.make_async_copy(
                hn_ref.at[pl.ds(t, 1)], xb_ref.at[pl.ds(dest_ref[t * TOP_K + kk], 1)], sem
            ).start(priority=kk % 2)
    for _ in range(TOP_K):
        pltpu.make_async_copy(hn_ref, xb_ref.at[pl.ds(0, tm)], sem).wait()


def _dispatch(hn, dest_flat, n_rows):
    T, D = hn.shape
    tm = DISPATCH_TILE
    return pl.pallas_call(
        _dispatch_kernel,
        grid=(T // tm,),
        in_specs=[pl.BlockSpec((tm * TOP_K,), lambda i: (i,), memory_space=pltpu.SMEM),
                  pl.BlockSpec((tm, D), lambda i: (i, 0))],
        out_specs=pl.BlockSpec(memory_space=pl.ANY),
        out_shape=jax.ShapeDtypeStruct((n_rows, D), F32),
        scratch_shapes=[pltpu.SemaphoreType.DMA(())],
        compiler_params=pltpu.CompilerParams(
            dimension_semantics=("arbitrary",), has_side_effects=True),
        name="moe_dispatch",
    )(dest_flat, hn)


def _expert_kernel(vblk_ref, vexp_ref, vfirst_ref, vnew_ref, gstart_ref, nvis_ref,
                   x_ref, wgu_ref, bgu_ref, wdn_ref, bdn_ref, y_ref, wgu_bf_ref, wdn_bf_ref):
    bm = x_ref.shape[0]
    ff = wdn_ref.shape[1]
    vis = pl.program_id(0)

    @pl.when(vis < nvis_ref[0])
    def _():
        @pl.when(vnew_ref[vis] == 1)
        def _():
            wgu_bf_ref[...] = wgu_ref[0].astype(BF16)
            wdn_bf_ref[...] = wdn_ref[0].astype(BF16)

        e = vexp_ref[vis]
        row0 = vblk_ref[vis] * bm
        row = lax.broadcasted_iota(jnp.int32, (bm, 1), 0) + row0
        mine = (row >= gstart_ref[e]) & (row < gstart_ref[e + 1])
        xb = x_ref[...].astype(BF16)
        gu = jnp.dot(xb, wgu_bf_ref[...], preferred_element_type=F32) + bgu_ref[0]
        g = jnp.minimum(gu[:, :ff], SWIGLU_LIMIT)
        up = jnp.clip(gu[:, ff:], -SWIGLU_LIMIT, SWIGLU_LIMIT)
        act = (up + 1.0) * (g * jax.nn.sigmoid(SWIGLU_ALPHA * g))
        y = jnp.dot(act.astype(BF16), wdn_bf_ref[...], preferred_element_type=F32) + bdn_ref[0]

        @pl.when(vfirst_ref[vis] == 1)
        def _():
            y_ref[...] = jnp.where(mine, y, 0.0)

        @pl.when(vfirst_ref[vis] == 0)
        def _():
            y_ref[...] = jnp.where(mine, y, y_ref[...])


def _experts(xb, visits, w_gu, b_gu, w_dn, b_dn):
    M, D = xb.shape
    E, _, ff2 = w_gu.shape
    bm = EXPERT_BLOCK
    vblk, vexp, vfirst, vnew, gstart, nvis = visits
    row_map = lambda v, vb, ve, vf, vn, gs, nv: (vb[v], 0)
    exp_map = lambda v, vb, ve, vf, vn, gs, nv: (ve[v], 0, 0)
    grid_spec = pltpu.PrefetchScalarGridSpec(
        num_scalar_prefetch=6,
        grid=(vblk.shape[0],),
        in_specs=[
            pl.BlockSpec((bm, D), row_map),
            pl.BlockSpec((1, D, ff2), exp_map),
            pl.BlockSpec((1, 1, ff2), exp_map),
            pl.BlockSpec((1, ff2 // 2, D), exp_map),
            pl.BlockSpec((1, 1, D), exp_map),
        ],
        out_specs=pl.BlockSpec((bm, D), row_map),
        scratch_shapes=[pltpu.VMEM((D, ff2), BF16), pltpu.VMEM((ff2 // 2, D), BF16)],
    )
    return pl.pallas_call(
        _expert_kernel,
        grid_spec=grid_spec,
        out_shape=jax.ShapeDtypeStruct((M, D), F32),
        compiler_params=pltpu.CompilerParams(
            dimension_semantics=("arbitrary",), vmem_limit_bytes=VMEM_LIMIT),
        name="moe_experts",
    )(vblk, vexp, vfirst, vnew, gstart, nvis, xb, w_gu, b_gu.reshape(E, 1, ff2), w_dn,
      b_dn.reshape(E, 1, D))


def _visit_schedule(counts, n_rows, bm):
    n_blocks = n_rows // bm
    max_visits = n_blocks + N_EXPERTS - 1
    ends = jnp.cumsum(counts)
    starts = ends - counts
    first_blk = starts // bm
    last_blk = jnp.maximum(ends - 1, 0) // bm
    n_vis = jnp.where(counts > 0, last_blk - first_blk + 1, 0)
    vis_end = jnp.cumsum(n_vis)
    vis_start = vis_end - n_vis
    total = vis_end[-1]
    v = jnp.minimum(jnp.arange(max_visits, dtype=jnp.int32), total - 1)
    vexp = jnp.sum(v[:, None] >= vis_end[None, :], axis=-1).astype(jnp.int32)
    pick = lambda table: jnp.sum(
        jnp.where(vexp[:, None] == jnp.arange(N_EXPERTS, dtype=jnp.int32), table[None, :], 0), axis=-1)
    vblk = (pick(first_blk) + v - pick(vis_start)).astype(jnp.int32)
    vfirst = jnp.concatenate([jnp.ones((1,), jnp.int32),
                              (vblk[1:] != vblk[:-1]).astype(jnp.int32)])
    vnew = jnp.concatenate([jnp.ones((1,), jnp.int32),
                            (vexp[1:] != vexp[:-1]).astype(jnp.int32)])
    gstart = jnp.concatenate([starts, ends[-1:]]).astype(jnp.int32)
    return vblk, vexp, vfirst, vnew, gstart, total.astype(jnp.int32).reshape(1)


def _combine_kernel(dest_ref, dest_next_ref, yb_ref, gate_ref, h_ref, gf_ref, out_ref,
                    buf_ref, sem):
    tm = h_ref.shape[0]
    step = pl.program_id(0)
    last = pl.num_programs(0) - 1

    def issue(idx_ref, slot):
        for t in range(tm):
            for kk in range(TOP_K):
                pltpu.make_async_copy(
                    yb_ref.at[pl.ds(idx_ref[t * TOP_K + kk], 1)],
                    buf_ref.at[slot, kk, pl.ds(t, 1)], sem.at[slot]
                ).start(priority=kk % 2)

    def reduce(slot):
        for kk in range(TOP_K):
            pltpu.make_async_copy(yb_ref.at[pl.ds(0, tm)], buf_ref.at[slot, kk], sem.at[slot]).wait()
        gates = gate_ref[...]
        y = h_ref[...]
        for kk in range(TOP_K):
            y = y + gates[:, kk:kk + 1] * buf_ref[slot, kk]
        out_ref[...] = _rms(y, gf_ref[...])

    @pl.when(step == 0)
    def _():
        issue(dest_ref, 0)

    for slot in range(2):
        @pl.when(step % 2 == slot)
        def _():
            @pl.when(step < last)
            def _():
                issue(dest_next_ref, 1 - slot)
            reduce(slot)


def _combine(yb, dest_flat, gates, h, final_g):
    T, D = h.shape
    tm = COMBINE_TILE
    n_steps = T // tm
    return pl.pallas_call(
        _combine_kernel,
        grid=(n_steps,),
        in_specs=[pl.BlockSpec((tm * TOP_K,), lambda i: (i,), memory_space=pltpu.SMEM),
                  pl.BlockSpec((tm * TOP_K,), lambda i: (jnp.minimum(i + 1, n_steps - 1),),
                               memory_space=pltpu.SMEM),
                  pl.BlockSpec(memory_space=pl.ANY),
                  pl.BlockSpec((tm, TOP_K), lambda i: (i, 0)),
                  pl.BlockSpec((tm, D), lambda i: (i, 0)),
                  pl.BlockSpec((1, D), lambda i: (0, 0))],
        out_specs=pl.BlockSpec((tm, D), lambda i: (i, 0)),
        out_shape=jax.ShapeDtypeStruct((T, D), F32),
        scratch_shapes=[pltpu.VMEM((2, TOP_K, tm, D), F32), pltpu.SemaphoreType.DMA((2,))],
        compiler_params=pltpu.CompilerParams(
            dimension_semantics=("arbitrary",), vmem_limit_bytes=VMEM_LIMIT),
        name="moe_combine",
    )(dest_flat, dest_flat, yb, gates, h, final_g.reshape(1, D))


def _rope_tables(seq_len):
    half = HEAD_DIM // 2
    inv_freq = 1.0 / (ROPE_THETA ** (jnp.arange(0, HEAD_DIM, 2, dtype=F32) / HEAD_DIM))
    ang = jnp.arange(seq_len, dtype=F32)[:, None] * inv_freq[None, :]
    cos = jnp.tile(jnp.cos(ang), (1, LANES // half))
    sin = jnp.tile(jnp.concatenate([-jnp.sin(ang), jnp.sin(ang)], axis=-1), (1, LANES // HEAD_DIM))
    return cos, sin


def _moe(hn, h, idx, gates, rank, counts, w_gu, b_gu, w_dn, b_dn, final_g):
    T, D = hn.shape
    n_assign = T * TOP_K
    counts = counts.reshape(N_EXPERTS)
    starts = jnp.cumsum(counts) - counts
    onehot = idx[..., None] == jnp.arange(N_EXPERTS, dtype=jnp.int32)
    dest = rank + jnp.sum(jnp.where(onehot, starts, 0), axis=-1)
    dest_flat = dest.T.reshape(n_assign).astype(jnp.int32)
    visits = _visit_schedule(counts, n_assign, EXPERT_BLOCK)

    xb = _dispatch(hn, dest_flat, n_assign)
    yb = _experts(xb, visits, w_gu, b_gu, w_dn, b_dn)
    return _combine(yb, dest_flat, gates.T, h, final_g)


def kernel(x, norm1_g, w_mix_in, attn_norm_g, conv_w, conv_norm_g, w_mix_out, norm2_g,
           w_router, b_router, w_gate_up, b_gate_up, w_down, b_down, final_norm_g):
    B, S, D = x.shape
    assert norm1_g.shape[0] == 1, "single-layer block"
    cos_tab, sin_tab = _rope_tables(S)
    nat, d4, d16, conv_n = _in_projection(x, norm1_g[0], w_mix_in[0].astype(BF16), conv_w[0],
                                          conv_norm_g[0], cos_tab, sin_tab)
    attn = _dilated_attention(nat, d4, d16)

    T = B * S
    h, hn, idx, gates, rank, counts = _out_projection_router(
        attn.reshape(T, ATT_WIDTH), conv_n.reshape(T, CONV_WIDTH), x.reshape(T, D),
        attn_norm_g[0], w_mix_out[0].astype(BF16), norm2_g[0], w_router[0], b_router[0])
    out = _moe(hn, h, idx, gates, rank, counts, w_gate_up[0], b_gate_up[0], w_down[0],
               b_down[0], final_norm_g)
    return out.reshape(B, S, D)
```

```python
import functools

import jax
import jax.numpy as jnp
from jax import lax
from jax.experimental import pallas as pl
from jax.experimental.pallas import tpu as pltpu

F32 = jnp.float32
BF16 = jnp.bfloat16

N_HEADS = 8
HEAD_DIM = 64
ATT_WIDTH = N_HEADS * HEAD_DIM
CONV_WIDTH = 512
ATT_BLOCK = 128
ROPE_THETA = 10000.0
N_EXPERTS = 32
TOP_K = 4
SWIGLU_ALPHA = 1.702
SWIGLU_LIMIT = 7.0
NORM_EPS = 1e-5
NEG_BIG = -1e30

LANES = 128
N_PAIRS = ATT_WIDTH // LANES
VMEM_LIMIT = 52 * 1024 * 1024

PROJ_TILE = 512
ATT_ROWS = 1024
ATT_CLASS_ROWS = 256
EXPERT_BLOCK = 512
DISPATCH_TILE = 512
COMBINE_TILE = 256


def _rms(x, g):
    return x * lax.rsqrt(jnp.mean(x * x, axis=-1, keepdims=True) + NORM_EPS) * g


def _inproj_kernel(x_ref, g1_ref, w_ref, cos_ref, sin_ref, convw_ref, convg_ref,
                   nat_ref, d4_ref, d16_ref, c_ref, zbuf_ref, sa_ref, sb_ref):
    tm = x_ref.shape[1]
    hn = _rms(x_ref[0], g1_ref[...]).astype(BF16)
    proj = jnp.dot(hn, w_ref[...], preferred_element_type=F32)

    cos = cos_ref[...]
    sin = sin_ref[...]
    lane = lax.broadcasted_iota(jnp.int32, (tm, LANES), 1)
    first_half = (lane % HEAD_DIM) < (HEAD_DIM // 2)

    def rope(t):
        partner = jnp.where(first_half, pltpu.roll(t, LANES - HEAD_DIM // 2, 1),
                            pltpu.roll(t, HEAD_DIM // 2, 1))
        return t * cos + partner * sin

    for ti in range(3):
        for c in range(N_PAIRS):
            col = ti * ATT_WIDTH + c * LANES
            val = proj[:, col:col + LANES]
            if ti == 0:
                val = rope(val) * (HEAD_DIM ** -0.5)
            elif ti == 1:
                val = rope(val)
            nat_ref[ti, c] = val.astype(BF16)
            slab = ti * N_PAIRS + c
            sa_ref[slab] = val
            for r in range(4):
                piece = sa_ref[slab, pl.ds(r, tm // 4, stride=4), :]
                d4_ref[ti, c, r] = piece.astype(BF16)
                sb_ref[slab, r] = piece
            for r4 in range(4):
                for j in range(4):
                    piece = sb_ref[slab, r4, pl.ds(j, tm // 16, stride=4), :]
                    d16_ref[ti, c, r4 + 4 * j] = piece.astype(BF16)

    base = 3 * ATT_WIDTH
    u = proj[:, base:base + CONV_WIDTH]
    gate_b = proj[:, base + CONV_WIDTH:base + 2 * CONV_WIDTH]
    gate_c = proj[:, base + 2 * CONV_WIDTH:base + 3 * CONV_WIDTH]
    z = gate_c * u

    @pl.when(pl.program_id(1) == 0)
    def _():
        zbuf_ref[0:8, :] = jnp.zeros((8, CONV_WIDTH), F32)

    @pl.when(pl.program_id(1) != 0)
    def _():
        zbuf_ref[0:8, :] = zbuf_ref[tm:tm + 8, :]

    zbuf_ref[8:8 + tm, :] = z
    z1 = zbuf_ref[7:7 + tm, :]
    z2 = zbuf_ref[6:6 + tm, :]
    conv = z2 * convw_ref[0:1, :] + z1 * convw_ref[1:2, :] + z * convw_ref[2:3, :]
    c_ref[0] = _rms(gate_b * conv, convg_ref[...]).astype(BF16)


def _in_projection(x, norm1_g, w_in, conv_w, conv_g, cos_tab, sin_tab):
    B, S, D = x.shape
    tm = PROJ_TILE
    width = w_in.shape[1]
    full = lambda shape: pl.BlockSpec(shape, lambda b, i: (0,) * len(shape))
    nat = jax.ShapeDtypeStruct((3, N_PAIRS, B, S, LANES), BF16)
    d4 = jax.ShapeDtypeStruct((3, N_PAIRS, B, 4, S // 4, LANES), BF16)
    d16 = jax.ShapeDtypeStruct((3, N_PAIRS, B, 16, S // 16, LANES), BF16)
    return pl.pallas_call(
        _inproj_kernel,
        grid=(B, S // tm),
        in_specs=[
            pl.BlockSpec((1, tm, D), lambda b, i: (b, i, 0)),
            full((1, D)),
            full((D, width)),
            pl.BlockSpec((tm, LANES), lambda b, i: (i, 0)),
            pl.BlockSpec((tm, LANES), lambda b, i: (i, 0)),
            full((3, CONV_WIDTH)),
            full((1, CONV_WIDTH)),
        ],
        out_specs=[
            pl.BlockSpec((3, N_PAIRS, None, tm, LANES), lambda b, i: (0, 0, b, i, 0)),
            pl.BlockSpec((3, N_PAIRS, None, 4, tm // 4, LANES), lambda b, i: (0, 0, b, 0, i, 0)),
            pl.BlockSpec((3, N_PAIRS, None, 16, tm // 16, LANES), lambda b, i: (0, 0, b, 0, i, 0)),
            pl.BlockSpec((1, tm, CONV_WIDTH), lambda b, i: (b, i, 0)),
        ],
        out_shape=[nat, d4, d16, jax.ShapeDtypeStruct((B, S, CONV_WIDTH), BF16)],
        scratch_shapes=[pltpu.VMEM((tm + 8, CONV_WIDTH), F32),
                        pltpu.VMEM((3 * N_PAIRS, tm, LANES), F32),
                        pltpu.VMEM((3 * N_PAIRS, 4, tm // 4, LANES), F32)],
        compiler_params=pltpu.CompilerParams(
            dimension_semantics=("arbitrary", "arbitrary"), vmem_limit_bytes=VMEM_LIMIT),
        name="inproj",
    )(x, norm1_g.reshape(1, D), w_in, cos_tab, sin_tab, conv_w, conv_g.reshape(1, CONV_WIDTH))


def _band_bias(is_first):
    blk = ATT_BLOCK
    qi = lax.broadcasted_iota(jnp.int32, (blk, 2 * blk), 0)
    kj = lax.broadcasted_iota(jnp.int32, (blk, 2 * blk), 1)
    band = (kj >= qi) & (kj <= qi + blk)
    if is_first is not None:
        band = band & ((kj >= blk) | jnp.logical_not(is_first))
    return jnp.where(band, 0.0, NEG_BIG).astype(F32)


def _pair_attention(q2, k2, v2, bias, head0):
    parts = []
    for hh in range(2):
        sel = head0 if hh == 0 else jnp.logical_not(head0)
        qm = jnp.where(sel, q2, jnp.zeros_like(q2))
        s = lax.dot_general(qm, k2, (((1,), (1,)), ((), ())), preferred_element_type=F32) + bias
        m = jnp.max(s, axis=-1, keepdims=True)
        p = jnp.exp(s - m)
        l = jnp.sum(p, axis=-1, keepdims=True)
        parts.append((m, l, jnp.dot(p.astype(BF16), v2, preferred_element_type=F32)))
    (m0, l0, o0), (m1, l1, o1) = parts
    shape = o0.shape
    pick = lambda a, b: jnp.where(head0, jnp.broadcast_to(a, shape), jnp.broadcast_to(b, shape))
    return pick(m0, m1), pick(l0, l1), jnp.where(head0, o0, o1)


def _finish(m, l, o_un):
    return o_un * (1.0 / l), m + jnp.log(l)


def _merge(o_old, lse_old, m, l, o_un, want_lse=True):
    mx = jnp.maximum(lse_old, m)
    wa = jnp.exp(lse_old - mx)
    eb = jnp.exp(m - mx)
    tot = wa + eb * l
    o = (wa * o_old + eb * o_un) * (1.0 / tot)
    return o, (mx + jnp.log(tot) if want_lse else None)


def _attn_classes_kernel(*refs, has_prev):
    if has_prev:
        q_ref, kp_ref, kc_ref, vp_ref, vc_ref, op_ref, lp_ref, o_ref, lse_ref = refs
    else:
        q_ref, kp_ref, kc_ref, vp_ref, vc_ref, o_ref, lse_ref = refs
    blk = ATT_BLOCK
    rows_total = q_ref.shape[2]
    bias_band = _band_bias(None)
    bias_first = _band_bias(pl.program_id(1 if has_prev else 2) == 0)
    head0 = lax.broadcasted_iota(jnp.int32, (blk, LANES), 1) < HEAD_DIM
    for cls in range(4):
        for j in range(rows_total // blk):
            rows = slice(j * blk, (j + 1) * blk)
            for c in range(N_PAIRS):
                if j == 0:
                    k2 = jnp.concatenate([kp_ref[c, cls], kc_ref[c, cls, 0:blk]], axis=0)
                    v2 = jnp.concatenate([vp_ref[c, cls], vc_ref[c, cls, 0:blk]], axis=0)
                else:
                    k2 = kc_ref[c, cls, (j - 1) * blk:(j + 1) * blk]
                    v2 = vc_ref[c, cls, (j - 1) * blk:(j + 1) * blk]
                m, l, o_un = _pair_attention(q_ref[c, cls, rows], k2, v2,
                                             bias_first if j == 0 else bias_band, head0)
                if has_prev:
                    o_pair, lse_pair = _merge(op_ref[c, cls, rows], lp_ref[c, cls, rows], m, l, o_un)
                else:
                    o_pair, lse_pair = _finish(m, l, o_un)
                out_rows = pl.ds(cls + 4 * blk * j, blk, stride=4)
                o_ref[c, out_rows, :] = o_pair
                lse_ref[c, out_rows, :] = lse_pair


def _attn_natural_kernel(q_ref, kp_ref, kc_ref, vp_ref, vc_ref, op_ref, lp_ref, o_ref):
    blk = ATT_BLOCK
    rows_total = q_ref.shape[1]
    bias_band = _band_bias(None)
    bias_first = _band_bias(pl.program_id(1) == 0)
    head0 = lax.broadcasted_iota(jnp.int32, (blk, LANES), 1) < HEAD_DIM
    for j in range(rows_total // blk):
        rows = slice(j * blk, (j + 1) * blk)
        for c in range(N_PAIRS):
            if j == 0:
                k2 = jnp.concatenate([kp_ref[c], kc_ref[c, 0:blk]], axis=0)
                v2 = jnp.concatenate([vp_ref[c], vc_ref[c, 0:blk]], axis=0)
            else:
                k2 = kc_ref[c, (j - 1) * blk:(j + 1) * blk]
                v2 = vc_ref[c, (j - 1) * blk:(j + 1) * blk]
            m, l, o_un = _pair_attention(q_ref[c, rows], k2, v2,
                                         bias_first if j == 0 else bias_band, head0)
            o_pair, _ = _merge(op_ref[c, rows], lp_ref[c, rows], m, l, o_un, want_lse=False)
            o_ref[rows, c * LANES:(c + 1) * LANES] = o_pair


def _dilated_attention(nat, d4, d16):
    _, _, B, S, _ = nat.shape
    blk = ATT_BLOCK
    cparams = lambda n: pltpu.CompilerParams(
        dimension_semantics=("arbitrary",) * n, vmem_limit_bytes=VMEM_LIMIT)

    L16, L4 = S // 16, S // 4
    d16v = d16.reshape(3, N_PAIRS, B, 4, 4, L16, LANES)
    crows = ATT_CLASS_ROWS
    cper = crows // blk
    assert L16 % crows == 0 and S % ATT_ROWS == 0, "sequence too short for the attention tiling"
    def spec16(t, prev):
        if prev:
            return pl.BlockSpec((None, N_PAIRS, None, 4, None, blk, LANES),
                                lambda b, r4, n: (t, 0, b, 0, r4, jnp.maximum(n * cper - 1, 0), 0))
        return pl.BlockSpec((None, N_PAIRS, None, 4, None, crows, LANES),
                            lambda b, r4, n: (t, 0, b, 0, r4, n, 0))
    state4 = jax.ShapeDtypeStruct((N_PAIRS, B, 4, L4, LANES), F32)
    out16 = pl.BlockSpec((N_PAIRS, None, None, 4 * crows, LANES), lambda b, r4, n: (0, b, r4, n, 0))
    o4, l4 = pl.pallas_call(
        functools.partial(_attn_classes_kernel, has_prev=False),
        grid=(B, 4, L16 // crows),
        in_specs=[spec16(0, False), spec16(1, True), spec16(1, False), spec16(2, True),
                  spec16(2, False)],
        out_specs=[out16, out16],
        out_shape=[state4, state4],
        compiler_params=cparams(3),
        name="dilated_attn_d16",
    )(d16v, d16v, d16v, d16v, d16v)

    def spec4(t, prev):
        if prev:
            return pl.BlockSpec((None, N_PAIRS, None, 4, blk, LANES),
                                lambda b, n: (t, 0, b, 0, jnp.maximum(n * cper - 1, 0), 0))
        return pl.BlockSpec((None, N_PAIRS, None, 4, crows, LANES), lambda b, n: (t, 0, b, 0, n, 0))
    in_state4 = pl.BlockSpec((N_PAIRS, None, 4, crows, LANES), lambda b, n: (0, b, 0, n, 0))
    state1 = jax.ShapeDtypeStruct((N_PAIRS, B, S, LANES), F32)
    out4 = pl.BlockSpec((N_PAIRS, None, 4 * crows, LANES), lambda b, n: (0, b, n, 0))
    o1, l1 = pl.pallas_call(
        functools.partial(_attn_classes_kernel, has_prev=True),
        grid=(B, L4 // crows),
        in_specs=[spec4(0, False), spec4(1, True), spec4(1, False), spec4(2, True),
                  spec4(2, False), in_state4, in_state4],
        out_specs=[out4, out4],
        out_shape=[state1, state1],
        compiler_params=cparams(2),
        name="dilated_attn_d4",
    )(d4, d4, d4, d4, d4, o4, l4)

    rows = ATT_ROWS
    per = rows // blk
    cur1 = lambda t: pl.BlockSpec((None, N_PAIRS, None, rows, LANES), lambda b, n: (t, 0, b, n, 0))
    prev1 = lambda t: pl.BlockSpec((None, N_PAIRS, None, blk, LANES),
                                   lambda b, n: (t, 0, b, jnp.maximum(n * per - 1, 0), 0))
    in_state1 = pl.BlockSpec((N_PAIRS, None, rows, LANES), lambda b, n: (0, b, n, 0))
    return pl.pallas_call(
        _attn_natural_kernel,
        grid=(B, S // rows),
        in_specs=[cur1(0), prev1(1), cur1(1), prev1(2), cur1(2), in_state1, in_state1],
        out_specs=pl.BlockSpec((None, rows, ATT_WIDTH), lambda b, n: (b, n, 0)),
        out_shape=jax.ShapeDtypeStruct((B, S, ATT_WIDTH), F32),
        compiler_params=cparams(2),
        name="dilated_attn_d1",
    )(nat, nat, nat, nat, nat, o1, l1)


def _split3(t):
    hi = t.astype(BF16)
    r1 = t - hi.astype(F32)
    mid = r1.astype(BF16)
    lo = (r1 - mid.astype(F32)).astype(BF16)
    return hi, mid, lo


def _outproj_router_kernel(attn_ref, conv_ref, x_ref, ga_ref, wo_ref, g2_ref, wrt_ref, br_ref,
                           h_ref, hn_ref, idx_ref, gate_ref, rank_ref, cnt_ref, carry_ref):
    tm = x_ref.shape[0]
    step = pl.program_id(0)

    @pl.when(step == 0)
    def _():
        carry_ref[...] = jnp.zeros_like(carry_ref)

    attn_n = _rms(attn_ref[...], ga_ref[...]).astype(BF16)
    mixed = jnp.dot(attn_n, wo_ref[0:ATT_WIDTH, :], preferred_element_type=F32)
    mixed += jnp.dot(conv_ref[...], wo_ref[ATT_WIDTH:, :], preferred_element_type=F32)
    h = x_ref[...] + mixed
    h_ref[...] = h
    hn = _rms(h, g2_ref[...])
    hn_ref[...] = hn

    a0, a1, a2 = _split3(hn)
    dot_t = lambda w, a: lax.dot_general(w, a, (((1,), (1,)), ((), ())),
                                         preferred_element_type=F32)
    ne = N_EXPERTS
    p0 = dot_t(wrt_ref[...], a0)
    p1 = dot_t(wrt_ref[0:2 * ne, :], a1)
    p2 = dot_t(wrt_ref[0:ne, :], a2)
    logits = (p0[0:ne] + (p0[ne:2 * ne] + p1[0:ne])
              + (p0[2 * ne:] + p1[ne:] + p2)) + br_ref[...]

    eidx = lax.broadcasted_iota(jnp.int32, (N_EXPERTS, tm), 0)
    work = logits
    vals, idxs = [], []
    for _ in range(TOP_K):
        mval = jnp.max(work, axis=0, keepdims=True)
        midx = jnp.min(jnp.where(work == mval, eidx, N_EXPERTS), axis=0, keepdims=True)
        vals.append(mval)
        idxs.append(midx)
        work = jnp.where(eidx == midx, -jnp.inf, work)
    exps = [jnp.exp(v - vals[0]) for v in vals]
    denom = exps[0] + exps[1] + exps[2] + exps[3]

    chosen = jnp.zeros((N_EXPERTS, tm), F32)
    for midx in idxs:
        chosen = chosen + (eidx == midx).astype(F32)
    si = lax.broadcasted_iota(jnp.int32, (tm, tm), 0)
    ti = lax.broadcasted_iota(jnp.int32, (tm, tm), 1)
    earlier = (si < ti).astype(BF16)
    before = jnp.dot(chosen.astype(BF16), earlier, preferred_element_type=F32) + carry_ref[...]
    for kk in range(TOP_K):
        idx_ref[kk:kk + 1, :] = idxs[kk]
        gate_ref[kk:kk + 1, :] = exps[kk] / denom
        rank_ref[kk:kk + 1, :] = jnp.sum(
            jnp.where(eidx == idxs[kk], before, 0.0), axis=0, keepdims=True).astype(jnp.int32)
    carry_ref[...] = carry_ref[...] + jnp.sum(chosen, axis=1, keepdims=True)
    cnt_ref[...] = carry_ref[...].astype(jnp.int32)


def _out_projection_router(attn, conv_n, x, attn_g, w_out, norm2_g, w_router, b_router):
    T, D = x.shape
    tm = PROJ_TILE
    tok = lambda width: pl.BlockSpec((tm, width), lambda i: (i, 0))
    per_k = pl.BlockSpec((TOP_K, tm), lambda i: (0, i))
    full = lambda shape: pl.BlockSpec(shape, lambda i: (0,) * len(shape))
    wrt_pieces = jnp.concatenate(_split3(w_router.T), axis=0)
    return pl.pallas_call(
        _outproj_router_kernel,
        grid=(T // tm,),
        in_specs=[tok(ATT_WIDTH), tok(CONV_WIDTH), tok(D), full((1, ATT_WIDTH)),
                  full((ATT_WIDTH + CONV_WIDTH, D)), full((1, D)),
                  full((3 * N_EXPERTS, D)), full((N_EXPERTS, 1))],
        out_specs=[tok(D), tok(D), per_k, per_k, per_k, full((N_EXPERTS, 1))],
        out_shape=[jax.ShapeDtypeStruct((T, D), F32), jax.ShapeDtypeStruct((T, D), F32),
                   jax.ShapeDtypeStruct((TOP_K, T), jnp.int32),
                   jax.ShapeDtypeStruct((TOP_K, T), F32),
                   jax.ShapeDtypeStruct((TOP_K, T), jnp.int32),
                   jax.ShapeDtypeStruct((N_EXPERTS, 1), jnp.int32)],
        scratch_shapes=[pltpu.VMEM((N_EXPERTS, 1), F32)],
        compiler_params=pltpu.CompilerParams(
            dimension_semantics=("arbitrary",), vmem_limit_bytes=VMEM_LIMIT),
        name="outproj_router",
    )(attn, conv_n, x, attn_g.reshape(1, -1), w_out, norm2_g.reshape(1, D),
      wrt_pieces, b_router.reshape(N_EXPERTS, 1))


def _dispatch_kernel(dest_ref, hn_ref, xb_ref, sem):
    tm = hn_ref.shape[0]

    for t in range(tm):
        for kk in range(TOP_K):
            pltpu.make_async_copy(
                hn_ref.at[pl.ds(t, 1)], xb_ref.at[pl.ds(dest_ref[t * TOP_K + kk], 1)], sem
            ).start(priority=kk % 2)
    for _ in range(TOP_K):
        pltpu.make_async_copy(hn_ref, xb_ref.at[pl.ds(0, tm)], sem).wait()


def _dispatch(hn, dest_flat, n_rows):
    T, D = hn.shape
    tm = DISPATCH_TILE
    return pl.pallas_call(
        _dispatch_kernel,
        grid=(T // tm,),
        in_specs=[pl.BlockSpec((tm * TOP_K,), lambda i: (i,), memory_space=pltpu.SMEM),
                  pl.BlockSpec((tm, D), lambda i: (i, 0))],
        out_specs=pl.BlockSpec(memory_space=pl.ANY),
        out_shape=jax.ShapeDtypeStruct((n_rows, D), F32),
        scratch_shapes=[pltpu.SemaphoreType.DMA(())],
        compiler_params=pltpu.CompilerParams(
            dimension_semantics=("arbitrary",), has_side_effects=True),
        name="moe_dispatch",
    )(dest_flat, hn)


def _expert_kernel(vblk_ref, vexp_ref, vnew_ref, nvis_ref,
                   x_ref, wgu_ref, bgu_ref, wdn_ref, bdn_ref, y_ref, wgu_bf_ref, wdn_bf_ref):
    ff = wdn_ref.shape[1]
    vis = pl.program_id(0)

    @pl.when(vis < nvis_ref[0])
    def _():
        @pl.when(vnew_ref[vis] == 1)
        def _():
            wgu_bf_ref[...] = wgu_ref[0].astype(BF16)
            wdn_bf_ref[...] = wdn_ref[0].astype(BF16)

        xb = x_ref[...].astype(BF16)
        gu = jnp.dot(xb, wgu_bf_ref[...], preferred_element_type=F32) + bgu_ref[0]
        g = jnp.minimum(gu[:, :ff], SWIGLU_LIMIT)
        up = jnp.clip(gu[:, ff:], -SWIGLU_LIMIT, SWIGLU_LIMIT)
        act = (up + 1.0) * (g * jax.nn.sigmoid(SWIGLU_ALPHA * g))
        y_ref[...] = (jnp.dot(act.astype(BF16), wdn_bf_ref[...], preferred_element_type=F32)
                      + bdn_ref[0])

    @pl.when(vis >= nvis_ref[0])
    def _():
        y_ref[...] = jnp.zeros_like(y_ref)


def _experts(xb, visits, w_gu, b_gu, w_dn, b_dn):
    M, D = xb.shape
    E, _, ff2 = w_gu.shape
    bm = EXPERT_BLOCK
    vblk, vexp, vnew, nvis = visits
    row_map = lambda v, vb, ve, vn, nv: (vb[v], 0)
    exp_map = lambda v, vb, ve, vn, nv: (ve[v], 0, 0)
    grid_spec = pltpu.PrefetchScalarGridSpec(
        num_scalar_prefetch=4,
        grid=(vblk.shape[0],),
        in_specs=[
            pl.BlockSpec((bm, D), row_map),
            pl.BlockSpec((1, D, ff2), exp_map),
            pl.BlockSpec((1, 1, ff2), exp_map),
            pl.BlockSpec((1, ff2 // 2, D), exp_map),
            pl.BlockSpec((1, 1, D), exp_map),
        ],
        out_specs=pl.BlockSpec((bm, D), lambda v, vb, ve, vn, nv: (v, 0)),
        scratch_shapes=[pltpu.VMEM((D, ff2), BF16), pltpu.VMEM((ff2 // 2, D), BF16)],
    )
    return pl.pallas_call(
        _expert_kernel,
        grid_spec=grid_spec,
        out_shape=jax.ShapeDtypeStruct((vblk.shape[0] * bm, D), F32),
        compiler_params=pltpu.CompilerParams(
            dimension_semantics=("arbitrary",), vmem_limit_bytes=VMEM_LIMIT),
        name="moe_experts",
    )(vblk, vexp, vnew, nvis, xb, w_gu, b_gu.reshape(E, 1, ff2), w_dn,
      b_dn.reshape(E, 1, D))


def _visit_schedule(counts, n_rows, bm):
    n_blocks = n_rows // bm
    max_visits = n_blocks + N_EXPERTS - 1
    ends = jnp.cumsum(counts)
    starts = ends - counts
    first_blk = starts // bm
    last_blk = jnp.maximum(ends - 1, 0) // bm
    n_vis = jnp.where(counts > 0, last_blk - first_blk + 1, 0)
    vis_end = jnp.cumsum(n_vis)
    vis_start = vis_end - n_vis
    total = vis_end[-1]
    v = jnp.minimum(jnp.arange(max_visits, dtype=jnp.int32), total - 1)
    vexp = jnp.sum(v[:, None] >= vis_end[None, :], axis=-1).astype(jnp.int32)
    pick = lambda table: jnp.sum(
        jnp.where(vexp[:, None] == jnp.arange(N_EXPERTS, dtype=jnp.int32), table[None, :], 0), axis=-1)
    vblk = (pick(first_blk) + v - pick(vis_start)).astype(jnp.int32)
    vnew = jnp.concatenate([jnp.ones((1,), jnp.int32),
                            (vexp[1:] != vexp[:-1]).astype(jnp.int32)])
    tables = (vblk, vexp, vnew, total.astype(jnp.int32).reshape(1))
    return tables, (vis_start - first_blk).astype(jnp.int32)


def _combine_kernel(dest_ref, dest_next_ref, yb_ref, gate_ref, h_ref, gf_ref, out_ref,
                    buf_ref, sem):
    tm = h_ref.shape[0]
    step = pl.program_id(0)
    last = pl.num_programs(0) - 1

    def issue(idx_ref, slot):
        for t in range(tm):
            for kk in range(TOP_K):
                pltpu.make_async_copy(
                    yb_ref.at[pl.ds(idx_ref[t * TOP_K + kk], 1)],
                    buf_ref.at[slot, kk, pl.ds(t, 1)], sem.at[slot]
                ).start(priority=kk % 2)

    def reduce(slot):
        for kk in range(TOP_K):
            pltpu.make_async_copy(yb_ref.at[pl.ds(0, tm)], buf_ref.at[slot, kk], sem.at[slot]).wait()
        gates = gate_ref[...]
        y = h_ref[...]
        for kk in range(TOP_K):
            y = y + gates[:, kk:kk + 1] * buf_ref[slot, kk]
        out_ref[...] = _rms(y, gf_ref[...])

    @pl.when(step == 0)
    def _():
        issue(dest_ref, 0)

    for slot in range(2):
        @pl.when(step % 2 == slot)
        def _():
            @pl.when(step < last)
            def _():
                issue(dest_next_ref, 1 - slot)
            reduce(slot)


def _combine(yb, dest_flat, gates, h, final_g):
    T, D = h.shape
    tm = COMBINE_TILE
    n_steps = T // tm
    return pl.pallas_call(
        _combine_kernel,
        grid=(n_steps,),
        in_specs=[pl.BlockSpec((tm * TOP_K,), lambda i: (i,), memory_space=pltpu.SMEM),
                  pl.BlockSpec((tm * TOP_K,), lambda i: (jnp.minimum(i + 1, n_steps - 1),),
                               memory_space=pltpu.SMEM),
                  pl.BlockSpec(memory_space=pl.ANY),
                  pl.BlockSpec((tm, TOP_K), lambda i: (i, 0)),
                  pl.BlockSpec((tm, D), lambda i: (i, 0)),
                  pl.BlockSpec((1, D), lambda i: (0, 0))],
        out_specs=pl.BlockSpec((tm, D), lambda i: (i, 0)),
        out_shape=jax.ShapeDtypeStruct((T, D), F32),
        scratch_shapes=[pltpu.VMEM((2, TOP_K, tm, D), F32), pltpu.SemaphoreType.DMA((2,))],
        compiler_params=pltpu.CompilerParams(
            dimension_semantics=("arbitrary",), vmem_limit_bytes=VMEM_LIMIT),
        name="moe_combine",
    )(dest_flat, dest_flat, yb, gates, h, final_g.reshape(1, D))


def _rope_tables(seq_len):
    half = HEAD_DIM // 2
    inv_freq = 1.0 / (ROPE_THETA ** (jnp.arange(0, HEAD_DIM, 2, dtype=F32) / HEAD_DIM))
    ang = jnp.arange(seq_len, dtype=F32)[:, None] * inv_freq[None, :]
    cos = jnp.tile(jnp.cos(ang), (1, LANES // half))
    sin = jnp.tile(jnp.concatenate([-jnp.sin(ang), jnp.sin(ang)], axis=-1), (1, LANES // HEAD_DIM))
    return cos, sin


def _moe(hn, h, idx, gates, rank, counts, w_gu, b_gu, w_dn, b_dn, final_g):
    T, D = hn.shape
    n_assign = T * TOP_K
    counts = counts.reshape(N_EXPERTS)
    starts = jnp.cumsum(counts) - counts
    onehot = idx[..., None] == jnp.arange(N_EXPERTS, dtype=jnp.int32)
    dest = rank + jnp.sum(jnp.where(onehot, starts, 0), axis=-1)
    flat = lambda a: a.T.reshape(n_assign).astype(jnp.int32)
    bm = EXPERT_BLOCK
    visits, visit_offset = _visit_schedule(counts, n_assign, bm)
    visit = dest // bm + jnp.sum(jnp.where(onehot, visit_offset, 0), axis=-1)
    src = visit * bm + dest % bm

    xb = _dispatch(hn, flat(dest), n_assign)
    yb = _experts(xb, visits, w_gu, b_gu, w_dn, b_dn)
    return _combine(yb, flat(src), gates.T, h, final_g)


def kernel(x, norm1_g, w_mix_in, attn_norm_g, conv_w, conv_norm_g, w_mix_out, norm2_g,
           w_router, b_router, w_gate_up, b_gate_up, w_down, b_down, final_norm_g):
    B, S, D = x.shape
    assert norm1_g.shape[0] == 1, "single-layer block"
    cos_tab, sin_tab = _rope_tables(S)
    nat, d4, d16, conv_n = _in_projection(x, norm1_g[0], w_mix_in[0].astype(BF16), conv_w[0],
                                          conv_norm_g[0], cos_tab, sin_tab)
    attn = _dilated_attention(nat, d4, d16)

    T = B * S
    h, hn, idx, gates, rank, counts = _out_projection_router(
        attn.reshape(T, ATT_WIDTH), conv_n.reshape(T, CONV_WIDTH), x.reshape(T, D),
        attn_norm_g[0], w_mix_out[0].astype(BF16), norm2_g[0], w_router[0], b_router[0])
    out = _moe(hn, h, idx, gates, rank, counts, w_gate_up[0], b_gate_up[0], w_down[0],
               b_down[0], final_norm_g)
    return out.reshape(B, S, D)
```

```python
import functools

import jax
import jax.numpy as jnp
from jax import lax
from jax.experimental import pallas as pl
from jax.experimental.pallas import tpu as pltpu

F32 = jnp.float32
BF16 = jnp.bfloat16

N_HEADS = 8
HEAD_DIM = 64
ATT_WIDTH = N_HEADS * HEAD_DIM
CONV_WIDTH = 512
ATT_BLOCK = 128
ROPE_THETA = 10000.0
N_EXPERTS = 32
TOP_K = 4
SWIGLU_ALPHA = 1.702
SWIGLU_LIMIT = 7.0
NORM_EPS = 1e-5
NEG_BIG = -1e30

LANES = 128
N_PAIRS = ATT_WIDTH // LANES
VMEM_LIMIT = 52 * 1024 * 1024

PROJ_TILE = 512
ATT_ROWS = 2048
ATT_CLASS_ROWS = 512
EXPERT_BLOCK = 512
DISPATCH_TILE = 512
COMBINE_TILE = 256


def _rms(x, g):
    return x * lax.rsqrt(jnp.mean(x * x, axis=-1, keepdims=True) + NORM_EPS) * g


def _inproj_kernel(x_ref, g1_ref, w_ref, cos_ref, sin_ref, convw_ref, convg_ref,
                   nat_ref, d4_ref, d16_ref, c_ref, zbuf_ref, sa_ref, sb_ref):
    tm = x_ref.shape[1]

    @pl.when(pl.program_id(1) == 0)
    def _():
        zbuf_ref[tm:tm + 8, :] = jnp.zeros((8, CONV_WIDTH), F32)

    hn = _rms(x_ref[0], g1_ref[...]).astype(BF16)
    proj = jnp.dot(hn, w_ref[...], preferred_element_type=F32)

    cos = cos_ref[...]
    sin = sin_ref[...]
    lane = lax.broadcasted_iota(jnp.int32, (tm, LANES), 1)
    first_half = (lane % HEAD_DIM) < (HEAD_DIM // 2)

    def rope(t):
        partner = jnp.where(first_half, pltpu.roll(t, LANES - HEAD_DIM // 2, 1),
                            pltpu.roll(t, HEAD_DIM // 2, 1))
        return t * cos + partner * sin

    for ti in range(3):
        for c in range(N_PAIRS):
            col = 3 * CONV_WIDTH + ti * ATT_WIDTH + c * LANES
            val = proj[:, col:col + LANES]
            if ti == 0:
                val = rope(val) * (HEAD_DIM ** -0.5)
            elif ti == 1:
                val = rope(val)
            nat_ref[ti, c] = val.astype(BF16)
            slab = ti * N_PAIRS + c
            sa_ref[slab] = val
            for r in range(4):
                piece = sa_ref[slab, pl.ds(r, tm // 4, stride=4), :]
                d4_ref[ti, c, r] = piece.astype(BF16)
                sb_ref[slab, r] = piece
            for r4 in range(4):
                for j in range(4):
                    piece = sb_ref[slab, r4, pl.ds(j, tm // 16, stride=4), :]
                    d16_ref[ti, c, r4 + 4 * j] = piece.astype(BF16)

    u = proj[:, 0:CONV_WIDTH]
    gate_b = proj[:, CONV_WIDTH:2 * CONV_WIDTH]
    gate_c = proj[:, 2 * CONV_WIDTH:3 * CONV_WIDTH]
    z = gate_c * u

    zbuf_ref[0:8, :] = zbuf_ref[tm:tm + 8, :]
    zbuf_ref[8:8 + tm, :] = z
    z1 = zbuf_ref[7:7 + tm, :]
    z2 = zbuf_ref[6:6 + tm, :]
    conv = z2 * convw_ref[0:1, :] + z1 * convw_ref[1:2, :] + z * convw_ref[2:3, :]
    c_ref[0] = _rms(gate_b * conv, convg_ref[...]).astype(BF16)


def _in_projection(x, norm1_g, w_in, conv_w, conv_g, cos_tab, sin_tab):
    B, S, D = x.shape
    tm = PROJ_TILE
    width = w_in.shape[1]
    full = lambda shape: pl.BlockSpec(shape, lambda b, i: (0,) * len(shape))
    nat = jax.ShapeDtypeStruct((3, N_PAIRS, B, S, LANES), BF16)
    d4 = jax.ShapeDtypeStruct((3, N_PAIRS, B, 4, S // 4, LANES), BF16)
    d16 = jax.ShapeDtypeStruct((3, N_PAIRS, B, 16, S // 16, LANES), BF16)
    return pl.pallas_call(
        _inproj_kernel,
        grid=(B, S // tm),
        in_specs=[
            pl.BlockSpec((1, tm, D), lambda b, i: (b, i, 0)),
            full((1, D)),
            full((D, width)),
            pl.BlockSpec((tm, LANES), lambda b, i: (i, 0)),
            pl.BlockSpec((tm, LANES), lambda b, i: (i, 0)),
            full((3, CONV_WIDTH)),
            full((1, CONV_WIDTH)),
        ],
        out_specs=[
            pl.BlockSpec((3, N_PAIRS, None, tm, LANES), lambda b, i: (0, 0, b, i, 0)),
            pl.BlockSpec((3, N_PAIRS, None, 4, tm // 4, LANES), lambda b, i: (0, 0, b, 0, i, 0)),
            pl.BlockSpec((3, N_PAIRS, None, 16, tm // 16, LANES), lambda b, i: (0, 0, b, 0, i, 0)),
            pl.BlockSpec((1, tm, CONV_WIDTH), lambda b, i: (b, i, 0)),
        ],
        out_shape=[nat, d4, d16, jax.ShapeDtypeStruct((B, S, CONV_WIDTH), BF16)],
        scratch_shapes=[pltpu.VMEM((tm + 8, CONV_WIDTH), F32),
                        pltpu.VMEM((3 * N_PAIRS, tm, LANES), F32),
                        pltpu.VMEM((3 * N_PAIRS, 4, tm // 4, LANES), F32)],
        compiler_params=pltpu.CompilerParams(
            dimension_semantics=("arbitrary", "arbitrary"), vmem_limit_bytes=VMEM_LIMIT),
        name="inproj",
    )(x, norm1_g.reshape(1, D), w_in, cos_tab, sin_tab, conv_w, conv_g.reshape(1, CONV_WIDTH))


def _band_bias(is_first):
    blk = ATT_BLOCK
    qi = lax.broadcasted_iota(jnp.int32, (blk, 2 * blk), 0)
    kj = lax.broadcasted_iota(jnp.int32, (blk, 2 * blk), 1)
    band = (kj >= qi) & (kj <= qi + blk)
    if is_first is not None:
        band = band & ((kj >= blk) | jnp.logical_not(is_first))
    return jnp.where(band, 0.0, NEG_BIG).astype(F32)


def _pair_attention(q2, k2, v2, bias, head0):
    parts = []
    for hh in range(2):
        sel = head0 if hh == 0 else jnp.logical_not(head0)
        qm = jnp.where(sel, q2, jnp.zeros_like(q2))
        s = lax.dot_general(qm, k2, (((1,), (1,)), ((), ())), preferred_element_type=F32) + bias
        m = jnp.max(s, axis=-1, keepdims=True)
        p = jnp.exp(s - m)
        l = jnp.sum(p, axis=-1, keepdims=True)
        parts.append((m, l, jnp.dot(p.astype(BF16), v2, preferred_element_type=F32)))
    (m0, l0, o0), (m1, l1, o1) = parts
    shape = o0.shape
    pick = lambda a, b: jnp.where(head0, jnp.broadcast_to(a, shape), jnp.broadcast_to(b, shape))
    return pick(m0, m1), pick(l0, l1), jnp.where(head0, o0, o1)


def _finish(m, l, o_un):
    return o_un * (1.0 / l), m + jnp.log(l)


def _merge(o_old, lse_old, m, l, o_un, want_lse=True):
    mx = jnp.maximum(lse_old, m)
    wa = jnp.exp(lse_old - mx)
    eb = jnp.exp(m - mx)
    tot = wa + eb * l
    o = (wa * o_old + eb * o_un) * (1.0 / tot)
    return o, (mx + jnp.log(tot) if want_lse else None)


def _attn_classes_kernel(*refs, has_prev):
    if has_prev:
        q_ref, kp_ref, kc_ref, vp_ref, vc_ref, op_ref, lp_ref, o_ref, lse_ref = refs
    else:
        q_ref, kp_ref, kc_ref, vp_ref, vc_ref, o_ref, lse_ref = refs
    blk = ATT_BLOCK
    rows_total = q_ref.shape[2]
    bias_band = _band_bias(None)
    bias_first = _band_bias(pl.program_id(1 if has_prev else 2) == 0)
    head0 = lax.broadcasted_iota(jnp.int32, (blk, LANES), 1) < HEAD_DIM
    for cls in range(4):
        for j in range(rows_total // blk):
            rows = slice(j * blk, (j + 1) * blk)
            for c in range(N_PAIRS):
                if j == 0:
                    k2 = jnp.concatenate([kp_ref[c, cls], kc_ref[c, cls, 0:blk]], axis=0)
                    v2 = jnp.concatenate([vp_ref[c, cls], vc_ref[c, cls, 0:blk]], axis=0)
                else:
                    k2 = kc_ref[c, cls, (j - 1) * blk:(j + 1) * blk]
                    v2 = vc_ref[c, cls, (j - 1) * blk:(j + 1) * blk]
                m, l, o_un = _pair_attention(q_ref[c, cls, rows], k2, v2,
                                             bias_first if j == 0 else bias_band, head0)
                if has_prev:
                    o_pair, lse_pair = _merge(op_ref[c, cls, rows], lp_ref[c, cls, rows], m, l, o_un)
                else:
                    o_pair, lse_pair = _finish(m, l, o_un)
                out_rows = pl.ds(cls + 4 * blk * j, blk, stride=4)
                o_ref[c, out_rows, :] = o_pair
                lse_ref[c, out_rows, :] = lse_pair


def _attn_natural_kernel(q_ref, kp_ref, kc_ref, vp_ref, vc_ref, op_ref, lp_ref, o_ref):
    blk = ATT_BLOCK
    rows_total = q_ref.shape[1]
    bias_band = _band_bias(None)
    bias_first = _band_bias(pl.program_id(1) == 0)
    head0 = lax.broadcasted_iota(jnp.int32, (blk, LANES), 1) < HEAD_DIM
    for j in range(rows_total // blk):
        rows = slice(j * blk, (j + 1) * blk)
        for c in range(N_PAIRS):
            if j == 0:
                k2 = jnp.concatenate([kp_ref[c], kc_ref[c, 0:blk]], axis=0)
                v2 = jnp.concatenate([vp_ref[c], vc_ref[c, 0:blk]], axis=0)
            else:
                k2 = kc_ref[c, (j - 1) * blk:(j + 1) * blk]
                v2 = vc_ref[c, (j - 1) * blk:(j + 1) * blk]
            m, l, o_un = _pair_attention(q_ref[c, rows], k2, v2,
                                         bias_first if j == 0 else bias_band, head0)
            o_pair, _ = _merge(op_ref[c, rows], lp_ref[c, rows], m, l, o_un, want_lse=False)
            o_ref[rows, c * LANES:(c + 1) * LANES] = o_pair


def _dilated_attention(nat, d4, d16):
    _, _, B, S, _ = nat.shape
    blk = ATT_BLOCK
    cparams = lambda n: pltpu.CompilerParams(
        dimension_semantics=("arbitrary",) * n, vmem_limit_bytes=VMEM_LIMIT)

    L16, L4 = S // 16, S // 4
    d16v = d16.reshape(3, N_PAIRS, B, 4, 4, L16, LANES)
    crows = ATT_CLASS_ROWS
    cper = crows // blk
    assert L16 % crows == 0 and S % ATT_ROWS == 0, "sequence too short for the attention tiling"
    def spec16(t, prev):
        if prev:
            return pl.BlockSpec((None, N_PAIRS, None, 4, None, blk, LANES),
                                lambda b, r4, n: (t, 0, b, 0, r4, jnp.maximum(n * cper - 1, 0), 0))
        return pl.BlockSpec((None, N_PAIRS, None, 4, None, crows, LANES),
                            lambda b, r4, n: (t, 0, b, 0, r4, n, 0))
    state4 = jax.ShapeDtypeStruct((N_PAIRS, B, 4, L4, LANES), F32)
    out16 = pl.BlockSpec((N_PAIRS, None, None, 4 * crows, LANES), lambda b, r4, n: (0, b, r4, n, 0))
    o4, l4 = pl.pallas_call(
        functools.partial(_attn_classes_kernel, has_prev=False),
        grid=(B, 4, L16 // crows),
        in_specs=[spec16(0, False), spec16(1, True), spec16(1, False), spec16(2, True),
                  spec16(2, False)],
        out_specs=[out16, out16],
        out_shape=[state4, state4],
        compiler_params=cparams(3),
        name="dilated_attn_d16",
    )(d16v, d16v, d16v, d16v, d16v)

    def spec4(t, prev):
        if prev:
            return pl.BlockSpec((None, N_PAIRS, None, 4, blk, LANES),
                                lambda b, n: (t, 0, b, 0, jnp.maximum(n * cper - 1, 0), 0))
        return pl.BlockSpec((None, N_PAIRS, None, 4, crows, LANES), lambda b, n: (t, 0, b, 0, n, 0))
    in_state4 = pl.BlockSpec((N_PAIRS, None, 4, crows, LANES), lambda b, n: (0, b, 0, n, 0))
    state1 = jax.ShapeDtypeStruct((N_PAIRS, B, S, LANES), F32)
    out4 = pl.BlockSpec((N_PAIRS, None, 4 * crows, LANES), lambda b, n: (0, b, n, 0))
    o1, l1 = pl.pallas_call(
        functools.partial(_attn_classes_kernel, has_prev=True),
        grid=(B, L4 // crows),
        in_specs=[spec4(0, False), spec4(1, True), spec4(1, False), spec4(2, True),
                  spec4(2, False), in_state4, in_state4],
        out_specs=[out4, out4],
        out_shape=[state1, state1],
        compiler_params=cparams(2),
        name="dilated_attn_d4",
    )(d4, d4, d4, d4, d4, o4, l4)

    rows = ATT_ROWS
    per = rows // blk
    cur1 = lambda t: pl.BlockSpec((None, N_PAIRS, None, rows, LANES), lambda b, n: (t, 0, b, n, 0))
    prev1 = lambda t: pl.BlockSpec((None, N_PAIRS, None, blk, LANES),
                                   lambda b, n: (t, 0, b, jnp.maximum(n * per - 1, 0), 0))
    in_state1 = pl.BlockSpec((N_PAIRS, None, rows, LANES), lambda b, n: (0, b, n, 0))
    return pl.pallas_call(
        _attn_natural_kernel,
        grid=(B, S // rows),
        in_specs=[cur1(0), prev1(1), cur1(1), prev1(2), cur1(2), in_state1, in_state1],
        out_specs=pl.BlockSpec((None, rows, ATT_WIDTH), lambda b, n: (b, n, 0)),
        out_shape=jax.ShapeDtypeStruct((B, S, ATT_WIDTH), F32),
        compiler_params=cparams(2),
        name="dilated_attn_d1",
    )(nat, nat, nat, nat, nat, o1, l1)


def _split3(t):
    hi = t.astype(BF16)
    r1 = t - hi.astype(F32)
    mid = r1.astype(BF16)
    lo = (r1 - mid.astype(F32)).astype(BF16)
    return hi, mid, lo


def _outproj_router_kernel(attn_ref, conv_ref, x_ref, ga_ref, wo_ref, g2_ref, wrt_ref, br_ref,
                           h_ref, hn_ref, idx_ref, gate_ref, rank_ref, cnt_ref, carry_ref):
    tm = x_ref.shape[0]
    step = pl.program_id(0)

    @pl.when(step == 0)
    def _():
        carry_ref[...] = jnp.zeros_like(carry_ref)

    attn_n = _rms(attn_ref[...], ga_ref[...]).astype(BF16)
    mixed = jnp.dot(attn_n, wo_ref[0:ATT_WIDTH, :], preferred_element_type=F32)
    mixed += jnp.dot(conv_ref[...], wo_ref[ATT_WIDTH:, :], preferred_element_type=F32)
    h = x_ref[...] + mixed
    h_ref[...] = h
    hn = _rms(h, g2_ref[...])
    hn_ref[...] = hn

    a0, a1, a2 = _split3(hn)
    dot_t = lambda w, a: lax.dot_general(w, a, (((1,), (1,)), ((), ())),
                                         preferred_element_type=F32)
    ne = N_EXPERTS
    p0 = dot_t(wrt_ref[...], a0)
    p1 = dot_t(wrt_ref[0:2 * ne, :], a1)
    p2 = dot_t(wrt_ref[0:ne, :], a2)
    logits = (p0[0:ne] + (p0[ne:2 * ne] + p1[0:ne])
              + (p0[2 * ne:] + p1[ne:] + p2)) + br_ref[...]

    eidx = lax.broadcasted_iota(jnp.int32, (N_EXPERTS, tm), 0)
    work = logits
    vals, idxs = [], []
    for _ in range(TOP_K):
        mval = jnp.max(work, axis=0, keepdims=True)
        midx = jnp.min(jnp.where(work == mval, eidx, N_EXPERTS), axis=0, keepdims=True)
        vals.append(mval)
        idxs.append(midx)
        work = jnp.where(eidx == midx, -jnp.inf, work)
    exps = [jnp.exp(v - vals[0]) for v in vals]
    denom = exps[0] + exps[1] + exps[2] + exps[3]

    chosen = jnp.zeros((N_EXPERTS, tm), F32)
    for midx in idxs:
        chosen = chosen + (eidx == midx).astype(F32)
    si = lax.broadcasted_iota(jnp.int32, (tm, tm), 0)
    ti = lax.broadcasted_iota(jnp.int32, (tm, tm), 1)
    earlier = (si < ti).astype(BF16)
    before = jnp.dot(chosen.astype(BF16), earlier, preferred_element_type=F32) + carry_ref[...]
    for kk in range(TOP_K):
        idx_ref[kk:kk + 1, :] = idxs[kk]
        gate_ref[kk:kk + 1, :] = exps[kk] / denom
        rank_ref[kk:kk + 1, :] = jnp.sum(
            jnp.where(eidx == idxs[kk], before, 0.0), axis=0, keepdims=True).astype(jnp.int32)
    carry_ref[...] = carry_ref[...] + jnp.sum(chosen, axis=1, keepdims=True)
    cnt_ref[...] = carry_ref[...].astype(jnp.int32)


def _out_projection_router(attn, conv_n, x, attn_g, w_out, norm2_g, w_router, b_router):
    T, D = x.shape
    tm = PROJ_TILE
    tok = lambda width: pl.BlockSpec((tm, width), lambda i: (i, 0))
    per_k = pl.BlockSpec((TOP_K, tm), lambda i: (0, i))
    full = lambda shape: pl.BlockSpec(shape, lambda i: (0,) * len(shape))
    wrt_pieces = jnp.concatenate(_split3(w_router.T), axis=0)
    return pl.pallas_call(
        _outproj_router_kernel,
        grid=(T // tm,),
        in_specs=[tok(ATT_WIDTH), tok(CONV_WIDTH), tok(D), full((1, ATT_WIDTH)),
                  full((ATT_WIDTH + CONV_WIDTH, D)), full((1, D)),
                  full((3 * N_EXPERTS, D)), full((N_EXPERTS, 1))],
        out_specs=[tok(D), tok(D), per_k, per_k, per_k, full((N_EXPERTS, 1))],
        out_shape=[jax.ShapeDtypeStruct((T, D), F32), jax.ShapeDtypeStruct((T, D), F32),
                   jax.ShapeDtypeStruct((TOP_K, T), jnp.int32),
                   jax.ShapeDtypeStruct((TOP_K, T), F32),
                   jax.ShapeDtypeStruct((TOP_K, T), jnp.int32),
                   jax.ShapeDtypeStruct((N_EXPERTS, 1), jnp.int32)],
        scratch_shapes=[pltpu.VMEM((N_EXPERTS, 1), F32)],
        compiler_params=pltpu.CompilerParams(
            dimension_semantics=("arbitrary",), vmem_limit_bytes=VMEM_LIMIT),
        name="outproj_router",
    )(attn, conv_n, x, attn_g.reshape(1, -1), w_out, norm2_g.reshape(1, D),
      wrt_pieces, b_router.reshape(N_EXPERTS, 1))


def _dispatch_kernel(dest_ref, hn_ref, xb_ref, sem):
    tm = hn_ref.shape[0]

    for t in range(tm):
        for kk in range(TOP_K):
            pltpu.make_async_copy(
                hn_ref.at[pl.ds(t, 1)], xb_ref.at[pl.ds(dest_ref[t * TOP_K + kk], 1)], sem
            ).start(priority=kk % 2)
    for _ in range(TOP_K):
        pltpu.make_async_copy(hn_ref, xb_ref.at[pl.ds(0, tm)], sem).wait()


def _dispatch(hn, dest_flat, n_rows):
    T, D = hn.shape
    tm = DISPATCH_TILE
    return pl.pallas_call(
        _dispatch_kernel,
        grid=(T // tm,),
        in_specs=[pl.BlockSpec((tm * TOP_K,), lambda i: (i,), memory_space=pltpu.SMEM),
                  pl.BlockSpec((tm, D), lambda i: (i, 0))],
        out_specs=pl.BlockSpec(memory_space=pl.ANY),
        out_shape=jax.ShapeDtypeStruct((n_rows, D), F32),
        scratch_shapes=[pltpu.SemaphoreType.DMA(())],
        compiler_params=pltpu.CompilerParams(
            dimension_semantics=("arbitrary",), has_side_effects=True),
        name="moe_dispatch",
    )(dest_flat, hn)


def _expert_kernel(vblk_ref, vexp_ref, vnew_ref, nvis_ref,
                   x_ref, wgu_ref, bgu_ref, wdn_ref, bdn_ref, y_ref, wgu_bf_ref, wdn_bf_ref):
    ff = wdn_ref.shape[1]
    vis = pl.program_id(0)

    @pl.when(vis < nvis_ref[0])
    def _():
        @pl.when(vnew_ref[vis] == 1)
        def _():
            wgu_bf_ref[...] = wgu_ref[0].astype(BF16)
            wdn_bf_ref[...] = wdn_ref[0].astype(BF16)

        xb = x_ref[...].astype(BF16)
        gu = jnp.dot(xb, wgu_bf_ref[...], preferred_element_type=F32) + bgu_ref[0]
        g = jnp.minimum(gu[:, :ff], SWIGLU_LIMIT)
        up = jnp.clip(gu[:, ff:], -SWIGLU_LIMIT, SWIGLU_LIMIT)
        act = (up + 1.0) * (g * jax.nn.sigmoid(SWIGLU_ALPHA * g))
        y_ref[...] = (jnp.dot(act.astype(BF16), wdn_bf_ref[...], preferred_element_type=F32)
                      + bdn_ref[0])

    @pl.when(vis >= nvis_ref[0])
    def _():
        y_ref[...] = jnp.zeros_like(y_ref)


def _experts(xb, visits, w_gu, b_gu, w_dn, b_dn):
    M, D = xb.shape
    E, _, ff2 = w_gu.shape
    bm = EXPERT_BLOCK
    vblk, vexp, vnew, nvis = visits
    row_map = lambda v, vb, ve, vn, nv: (vb[v], 0)
    exp_map = lambda v, vb, ve, vn, nv: (ve[v], 0, 0)
    grid_spec = pltpu.PrefetchScalarGridSpec(
        num_scalar_prefetch=4,
        grid=(vblk.shape[0],),
        in_specs=[
            pl.BlockSpec((bm, D), row_map),
            pl.BlockSpec((1, D, ff2), exp_map),
            pl.BlockSpec((1, 1, ff2), exp_map),
            pl.BlockSpec((1, ff2 // 2, D), exp_map),
            pl.BlockSpec((1, 1, D), exp_map),
        ],
        out_specs=pl.BlockSpec((bm, D), lambda v, vb, ve, vn, nv: (v, 0)),
        scratch_shapes=[pltpu.VMEM((D, ff2), BF16), pltpu.VMEM((ff2 // 2, D), BF16)],
    )
    return pl.pallas_call(
        _expert_kernel,
        grid_spec=grid_spec,
        out_shape=jax.ShapeDtypeStruct((vblk.shape[0] * bm, D), F32),
        compiler_params=pltpu.CompilerParams(
            dimension_semantics=("arbitrary",), vmem_limit_bytes=VMEM_LIMIT),
        name="moe_experts",
    )(vblk, vexp, vnew, nvis, xb, w_gu, b_gu.reshape(E, 1, ff2), w_dn,
      b_dn.reshape(E, 1, D))


def _visit_schedule(counts, n_rows, bm):
    n_blocks = n_rows // bm
    max_visits = n_blocks + N_EXPERTS - 1
    ends = jnp.cumsum(counts)
    starts = ends - counts
    first_blk = starts // bm
    last_blk = jnp.maximum(ends - 1, 0) // bm
    n_vis = jnp.where(counts > 0, last_blk - first_blk + 1, 0)
    vis_end = jnp.cumsum(n_vis)
    vis_start = vis_end - n_vis
    total = vis_end[-1]
    v = jnp.minimum(jnp.arange(max_visits, dtype=jnp.int32), total - 1)
    vexp = jnp.sum(v[:, None] >= vis_end[None, :], axis=-1).astype(jnp.int32)
    pick = lambda table: jnp.sum(
        jnp.where(vexp[:, None] == jnp.arange(N_EXPERTS, dtype=jnp.int32), table[None, :], 0), axis=-1)
    vblk = (pick(first_blk) + v - pick(vis_start)).astype(jnp.int32)
    vnew = jnp.concatenate([jnp.ones((1,), jnp.int32),
                            (vexp[1:] != vexp[:-1]).astype(jnp.int32)])
    tables = (vblk, vexp, vnew, total.astype(jnp.int32).reshape(1))
    return tables, (vis_start - first_blk).astype(jnp.int32)


def _combine_kernel(dest_ref, dest_next_ref, yb_ref, gate_ref, h_ref, gf_ref, out_ref,
                    buf_ref, sem):
    tm = h_ref.shape[0]
    step = pl.program_id(0)
    last = pl.num_programs(0) - 1

    def issue(idx_ref, slot):
        for t in range(tm):
            for kk in range(TOP_K):
                pltpu.make_async_copy(
                    yb_ref.at[pl.ds(idx_ref[t * TOP_K + kk], 1)],
                    buf_ref.at[slot, kk, pl.ds(t, 1)], sem.at[slot]
                ).start(priority=kk % 2)

    def reduce(slot):
        for kk in range(TOP_K):
            pltpu.make_async_copy(yb_ref.at[pl.ds(0, tm)], buf_ref.at[slot, kk], sem.at[slot]).wait()
        gates = gate_ref[...]
        y = h_ref[...]
        for kk in range(TOP_K):
            y = y + gates[:, kk:kk + 1] * buf_ref[slot, kk]
        out_ref[...] = _rms(y, gf_ref[...])

    @pl.when(step == 0)
    def _():
        issue(dest_ref, 0)

    for slot in range(2):
        @pl.when(step % 2 == slot)
        def _():
            @pl.when(step < last)
            def _():
                issue(dest_next_ref, 1 - slot)
            reduce(slot)


def _combine(yb, dest_flat, gates, h, final_g):
    T, D = h.shape
    tm = COMBINE_TILE
    n_steps = T // tm
    return pl.pallas_call(
        _combine_kernel,
        grid=(n_steps,),
        in_specs=[pl.BlockSpec((tm * TOP_K,), lambda i: (i,), memory_space=pltpu.SMEM),
                  pl.BlockSpec((tm * TOP_K,), lambda i: (jnp.minimum(i + 1, n_steps - 1),),
                               memory_space=pltpu.SMEM),
                  pl.BlockSpec(memory_space=pl.ANY),
                  pl.BlockSpec((tm, TOP_K), lambda i: (i, 0)),
                  pl.BlockSpec((tm, D), lambda i: (i, 0)),
                  pl.BlockSpec((1, D), lambda i: (0, 0))],
        out_specs=pl.BlockSpec((tm, D), lambda i: (i, 0)),
        out_shape=jax.ShapeDtypeStruct((T, D), F32),
        scratch_shapes=[pltpu.VMEM((2, TOP_K, tm, D), F32), pltpu.SemaphoreType.DMA((2,))],
        compiler_params=pltpu.CompilerParams(
            dimension_semantics=("arbitrary",), vmem_limit_bytes=VMEM_LIMIT),
        name="moe_combine",
    )(dest_flat, dest_flat, yb, gates, h, final_g.reshape(1, D))


def _rope_tables(seq_len):
    half = HEAD_DIM // 2
    inv_freq = 1.0 / (ROPE_THETA ** (jnp.arange(0, HEAD_DIM, 2, dtype=F32) / HEAD_DIM))
    ang = jnp.arange(seq_len, dtype=F32)[:, None] * inv_freq[None, :]
    cos = jnp.tile(jnp.cos(ang), (1, LANES // half))
    sin = jnp.tile(jnp.concatenate([-jnp.sin(ang), jnp.sin(ang)], axis=-1), (1, LANES // HEAD_DIM))
    return cos, sin


def _moe(hn, h, idx, gates, rank, counts, w_gu, b_gu, w_dn, b_dn, final_g):
    T, D = hn.shape
    n_assign = T * TOP_K
    counts = counts.reshape(N_EXPERTS)
    starts = jnp.cumsum(counts) - counts
    onehot = idx[..., None] == jnp.arange(N_EXPERTS, dtype=jnp.int32)
    dest = rank + jnp.sum(jnp.where(onehot, starts, 0), axis=-1)
    flat = lambda a: a.T.reshape(n_assign).astype(jnp.int32)
    bm = EXPERT_BLOCK
    visits, visit_offset = _visit_schedule(counts, n_assign, bm)
    visit = dest // bm + jnp.sum(jnp.where(onehot, visit_offset, 0), axis=-1)
    src = visit * bm + dest % bm

    xb = _dispatch(hn, flat(dest), n_assign)
    yb = _experts(xb, visits, w_gu, b_gu, w_dn, b_dn)
    return _combine(yb, flat(src), gates.T, h, final_g)


def kernel(x, norm1_g, w_mix_in, attn_norm_g, conv_w, conv_norm_g, w_mix_out, norm2_g,
           w_router, b_router, w_gate_up, b_gate_up, w_down, b_down, final_norm_g):
    B, S, D = x.shape
    assert norm1_g.shape[0] == 1, "single-layer block"
    cos_tab, sin_tab = _rope_tables(S)
    w_in = w_mix_in[0].astype(BF16)
    w_in = jnp.concatenate([w_in[:, 3 * ATT_WIDTH:], w_in[:, :3 * ATT_WIDTH]], axis=1)
    nat, d4, d16, conv_n = _in_projection(x, norm1_g[0], w_in, conv_w[0],
                                          conv_norm_g[0], cos_tab, sin_tab)
    attn = _dilated_attention(nat, d4, d16)

    T = B * S
    h, hn, idx, gates, rank, counts = _out_projection_router(
        attn.reshape(T, ATT_WIDTH), conv_n.reshape(T, CONV_WIDTH), x.reshape(T, D),
        attn_norm_g[0], w_mix_out[0].astype(BF16), norm2_g[0], w_router[0], b_router[0])
    out = _moe(hn, h, idx, gates, rank, counts, w_gate_up[0], b_gate_up[0], w_down[0],
               b_down[0], final_norm_g)
    return out.reshape(B, S, D)
```

```python
import functools

import jax
import jax.numpy as jnp
from jax import lax
from jax.experimental import pallas as pl
from jax.experimental.pallas import tpu as pltpu

F32 = jnp.float32
BF16 = jnp.bfloat16

N_HEADS = 8
HEAD_DIM = 64
ATT_WIDTH = N_HEADS * HEAD_DIM
CONV_WIDTH = 512
ATT_BLOCK = 128
ROPE_THETA = 10000.0
N_EXPERTS = 32
TOP_K = 4
SWIGLU_ALPHA = 1.702
SWIGLU_LIMIT = 7.0
NORM_EPS = 1e-5
NEG_BIG = -1e30

LANES = 128
N_PAIRS = ATT_WIDTH // LANES
VMEM_LIMIT = 52 * 1024 * 1024

PROJ_TILE = 512
ATT_ROWS = 2048
ATT_CLASS_ROWS = 512
EXPERT_BLOCK = 512
DISPATCH_TILE = 512
COMBINE_TILE = 256


def _rms(x, g):
    return x * lax.rsqrt(jnp.mean(x * x, axis=-1, keepdims=True) + NORM_EPS) * g


def _inproj_kernel(x_ref, g1_ref, w_ref, cos_ref, sin_ref, convw_ref, convg_ref,
                   nat_ref, d4_ref, d16_ref, c_ref, zbuf_ref, sa_ref, sb_ref):
    tm = x_ref.shape[1]

    @pl.when(pl.program_id(1) == 0)
    def _():
        zbuf_ref[tm:tm + 8, :] = jnp.zeros((8, CONV_WIDTH), F32)

    hn = _rms(x_ref[0], g1_ref[...]).astype(BF16)
    proj = jnp.dot(hn, w_ref[...], preferred_element_type=F32)

    cos = cos_ref[...]
    sin = sin_ref[...]
    lane = lax.broadcasted_iota(jnp.int32, (tm, LANES), 1)
    first_half = (lane % HEAD_DIM) < (HEAD_DIM // 2)

    def rope(t):
        partner = jnp.where(first_half, pltpu.roll(t, LANES - HEAD_DIM // 2, 1),
                            pltpu.roll(t, HEAD_DIM // 2, 1))
        return t * cos + partner * sin

    for ti in range(3):
        for c in range(N_PAIRS):
            col = 3 * CONV_WIDTH + ti * ATT_WIDTH + c * LANES
            val = proj[:, col:col + LANES]
            if ti == 0:
                val = rope(val) * (HEAD_DIM ** -0.5)
            elif ti == 1:
                val = rope(val)
            nat_ref[ti, c] = val.astype(BF16)
            slab = ti * N_PAIRS + c
            sa_ref[slab] = val
            for r in range(4):
                piece = sa_ref[slab, pl.ds(r, tm // 4, stride=4), :]
                d4_ref[ti, c, r] = piece.astype(BF16)
                sb_ref[slab, r] = piece
            for r4 in range(4):
                for j in range(4):
                    piece = sb_ref[slab, r4, pl.ds(j, tm // 16, stride=4), :]
                    d16_ref[ti, c, r4 + 4 * j] = piece.astype(BF16)

    u = proj[:, 0:CONV_WIDTH]
    gate_b = proj[:, CONV_WIDTH:2 * CONV_WIDTH]
    gate_c = proj[:, 2 * CONV_WIDTH:3 * CONV_WIDTH]
    z = gate_c * u

    zbuf_ref[0:8, :] = zbuf_ref[tm:tm + 8, :]
    zbuf_ref[8:8 + tm, :] = z
    z1 = zbuf_ref[7:7 + tm, :]
    z2 = zbuf_ref[6:6 + tm, :]
    conv = z2 * convw_ref[0:1, :] + z1 * convw_ref[1:2, :] + z * convw_ref[2:3, :]
    c_ref[0] = _rms(gate_b * conv, convg_ref[...]).astype(BF16)


def _in_projection(x, norm1_g, w_in, conv_w, conv_g, cos_tab, sin_tab):
    B, S, D = x.shape
    tm = PROJ_TILE
    width = w_in.shape[1]
    full = lambda shape: pl.BlockSpec(shape, lambda b, i: (0,) * len(shape))
    nat = jax.ShapeDtypeStruct((3, N_PAIRS, B, S, LANES), BF16)
    d4 = jax.ShapeDtypeStruct((3, N_PAIRS, B, 4, S // 4, LANES), BF16)
    d16 = jax.ShapeDtypeStruct((3, N_PAIRS, B, 16, S // 16, LANES), BF16)
    return pl.pallas_call(
        _inproj_kernel,
        grid=(B, S // tm),
        in_specs=[
            pl.BlockSpec((1, tm, D), lambda b, i: (b, i, 0)),
            full((1, D)),
            full((D, width)),
            pl.BlockSpec((tm, LANES), lambda b, i: (i, 0)),
            pl.BlockSpec((tm, LANES), lambda b, i: (i, 0)),
            full((3, CONV_WIDTH)),
            full((1, CONV_WIDTH)),
        ],
        out_specs=[
            pl.BlockSpec((3, N_PAIRS, None, tm, LANES), lambda b, i: (0, 0, b, i, 0)),
            pl.BlockSpec((3, N_PAIRS, None, 4, tm // 4, LANES), lambda b, i: (0, 0, b, 0, i, 0)),
            pl.BlockSpec((3, N_PAIRS, None, 16, tm // 16, LANES), lambda b, i: (0, 0, b, 0, i, 0)),
            pl.BlockSpec((1, tm, CONV_WIDTH), lambda b, i: (b, i, 0)),
        ],
        out_shape=[nat, d4, d16, jax.ShapeDtypeStruct((B, S, CONV_WIDTH), BF16)],
        scratch_shapes=[pltpu.VMEM((tm + 8, CONV_WIDTH), F32),
                        pltpu.VMEM((3 * N_PAIRS, tm, LANES), F32),
                        pltpu.VMEM((3 * N_PAIRS, 4, tm // 4, LANES), F32)],
        compiler_params=pltpu.CompilerParams(
            dimension_semantics=("arbitrary", "arbitrary"), vmem_limit_bytes=VMEM_LIMIT),
        name="inproj",
    )(x, norm1_g.reshape(1, D), w_in, cos_tab, sin_tab, conv_w, conv_g.reshape(1, CONV_WIDTH))


def _band_bias(is_first):
    blk = ATT_BLOCK
    qi = lax.broadcasted_iota(jnp.int32, (blk, 2 * blk), 0)
    kj = lax.broadcasted_iota(jnp.int32, (blk, 2 * blk), 1)
    band = (kj >= qi) & (kj <= qi + blk)
    if is_first is not None:
        band = band & ((kj >= blk) | jnp.logical_not(is_first))
    return jnp.where(band, 0.0, NEG_BIG).astype(F32)


def _pair_attention(q2, k2, v2, bias, head0):
    parts = []
    for hh in range(2):
        sel = head0 if hh == 0 else jnp.logical_not(head0)
        qm = jnp.where(sel, q2, jnp.zeros_like(q2))
        s = lax.dot_general(qm, k2, (((1,), (1,)), ((), ())), preferred_element_type=F32) + bias
        m = jnp.max(s, axis=-1, keepdims=True)
        p = jnp.exp(s - m)
        l = jnp.sum(p, axis=-1, keepdims=True)
        parts.append((m, l, jnp.dot(p.astype(BF16), v2, preferred_element_type=F32)))
    (m0, l0, o0), (m1, l1, o1) = parts
    shape = o0.shape
    pick = lambda a, b: jnp.where(head0, jnp.broadcast_to(a, shape), jnp.broadcast_to(b, shape))
    return pick(m0, m1), pick(l0, l1), jnp.where(head0, o0, o1)


def _finish(m, l, o_un):
    return o_un * (1.0 / l), m + jnp.log(l)


def _merge(o_old, lse_old, m, l, o_un, want_lse=True):
    mx = jnp.maximum(lse_old, m)
    wa = jnp.exp(lse_old - mx)
    eb = jnp.exp(m - mx)
    tot = wa + eb * l
    o = (wa * o_old + eb * o_un) * (1.0 / tot)
    return o, (mx + jnp.log(tot) if want_lse else None)


def _attn_classes_kernel(*refs, has_prev):
    if has_prev:
        q_ref, kp_ref, kc_ref, vp_ref, vc_ref, op_ref, lp_ref, o_ref, lse_ref = refs
    else:
        q_ref, kp_ref, kc_ref, vp_ref, vc_ref, o_ref, lse_ref = refs
    blk = ATT_BLOCK
    rows_total = q_ref.shape[2]
    bias_band = _band_bias(None)
    bias_first = _band_bias(pl.program_id(1 if has_prev else 2) == 0)
    head0 = lax.broadcasted_iota(jnp.int32, (blk, LANES), 1) < HEAD_DIM
    for cls in range(4):
        for j in range(rows_total // blk):
            rows = slice(j * blk, (j + 1) * blk)
            for c in range(N_PAIRS):
                if j == 0:
                    k2 = jnp.concatenate([kp_ref[c, cls], kc_ref[c, cls, 0:blk]], axis=0)
                    v2 = jnp.concatenate([vp_ref[c, cls], vc_ref[c, cls, 0:blk]], axis=0)
                else:
                    k2 = kc_ref[c, cls, (j - 1) * blk:(j + 1) * blk]
                    v2 = vc_ref[c, cls, (j - 1) * blk:(j + 1) * blk]
                m, l, o_un = _pair_attention(q_ref[c, cls, rows], k2, v2,
                                             bias_first if j == 0 else bias_band, head0)
                if has_prev:
                    o_pair, lse_pair = _merge(op_ref[c, cls, rows], lp_ref[c, cls, rows], m, l, o_un)
                else:
                    o_pair, lse_pair = _finish(m, l, o_un)
                out_rows = pl.ds(cls + 4 * blk * j, blk, stride=4)
                o_ref[c, out_rows, :] = o_pair
                lse_ref[c, out_rows, :] = lse_pair


def _attn_natural_kernel(q_ref, kp_ref, kc_ref, vp_ref, vc_ref, op_ref, lp_ref, o_ref):
    blk = ATT_BLOCK
    rows_total = q_ref.shape[1]
    bias_band = _band_bias(None)
    bias_first = _band_bias(pl.program_id(1) == 0)
    head0 = lax.broadcasted_iota(jnp.int32, (blk, LANES), 1) < HEAD_DIM
    for j in range(rows_total // blk):
        rows = slice(j * blk, (j + 1) * blk)
        for c in range(N_PAIRS):
            if j == 0:
                k2 = jnp.concatenate([kp_ref[c], kc_ref[c, 0:blk]], axis=0)
                v2 = jnp.concatenate([vp_ref[c], vc_ref[c, 0:blk]], axis=0)
            else:
                k2 = kc_ref[c, (j - 1) * blk:(j + 1) * blk]
                v2 = vc_ref[c, (j - 1) * blk:(j + 1) * blk]
            m, l, o_un = _pair_attention(q_ref[c, rows], k2, v2,
                                         bias_first if j == 0 else bias_band, head0)
            o_pair, _ = _merge(op_ref[c, rows], lp_ref[c, rows], m, l, o_un, want_lse=False)
            o_ref[rows, c * LANES:(c + 1) * LANES] = o_pair


def _dilated_attention(nat, d4, d16):
    _, _, B, S, _ = nat.shape
    blk = ATT_BLOCK
    cparams = lambda n: pltpu.CompilerParams(
        dimension_semantics=("arbitrary",) * n, vmem_limit_bytes=VMEM_LIMIT)

    L16, L4 = S // 16, S // 4
    d16v = d16.reshape(3, N_PAIRS, B, 4, 4, L16, LANES)
    crows = ATT_CLASS_ROWS
    cper = crows // blk
    assert L16 % crows == 0 and S % ATT_ROWS == 0, "sequence too short for the attention tiling"
    def spec16(t, prev):
        if prev:
            return pl.BlockSpec((None, N_PAIRS, None, 4, None, blk, LANES),
                                lambda b, r4, n: (t, 0, b, 0, r4, jnp.maximum(n * cper - 1, 0), 0))
        return pl.BlockSpec((None, N_PAIRS, None, 4, None, crows, LANES),
                            lambda b, r4, n: (t, 0, b, 0, r4, n, 0))
    state4 = jax.ShapeDtypeStruct((N_PAIRS, B, 4, L4, LANES), F32)
    out16 = pl.BlockSpec((N_PAIRS, None, None, 4 * crows, LANES), lambda b, r4, n: (0, b, r4, n, 0))
    o4, l4 = pl.pallas_call(
        functools.partial(_attn_classes_kernel, has_prev=False),
        grid=(B, 4, L16 // crows),
        in_specs=[spec16(0, False), spec16(1, True), spec16(1, False), spec16(2, True),
                  spec16(2, False)],
        out_specs=[out16, out16],
        out_shape=[state4, state4],
        compiler_params=cparams(3),
        name="dilated_attn_d16",
    )(d16v, d16v, d16v, d16v, d16v)

    def spec4(t, prev):
        if prev:
            return pl.BlockSpec((None, N_PAIRS, None, 4, blk, LANES),
                                lambda b, n: (t, 0, b, 0, jnp.maximum(n * cper - 1, 0), 0))
        return pl.BlockSpec((None, N_PAIRS, None, 4, crows, LANES), lambda b, n: (t, 0, b, 0, n, 0))
    in_state4 = pl.BlockSpec((N_PAIRS, None, 4, crows, LANES), lambda b, n: (0, b, 0, n, 0))
    state1 = jax.ShapeDtypeStruct((N_PAIRS, B, S, LANES), F32)
    out4 = pl.BlockSpec((N_PAIRS, None, 4 * crows, LANES), lambda b, n: (0, b, n, 0))
    o1, l1 = pl.pallas_call(
        functools.partial(_attn_classes_kernel, has_prev=True),
        grid=(B, L4 // crows),
        in_specs=[spec4(0, False), spec4(1, True), spec4(1, False), spec4(2, True),
                  spec4(2, False), in_state4, in_state4],
        out_specs=[out4, out4],
        out_shape=[state1, state1],
        compiler_params=cparams(2),
        name="dilated_attn_d4",
    )(d4, d4, d4, d4, d4, o4, l4)

    rows = ATT_ROWS
    per = rows // blk
    cur1 = lambda t: pl.BlockSpec((None, N_PAIRS, None, rows, LANES), lambda b, n: (t, 0, b, n, 0))
    prev1 = lambda t: pl.BlockSpec((None, N_PAIRS, None, blk, LANES),
                                   lambda b, n: (t, 0, b, jnp.maximum(n * per - 1, 0), 0))
    in_state1 = pl.BlockSpec((N_PAIRS, None, rows, LANES), lambda b, n: (0, b, n, 0))
    return pl.pallas_call(
        _attn_natural_kernel,
        grid=(B, S // rows),
        in_specs=[cur1(0), prev1(1), cur1(1), prev1(2), cur1(2), in_state1, in_state1],
        out_specs=pl.BlockSpec((None, rows, ATT_WIDTH), lambda b, n: (b, n, 0)),
        out_shape=jax.ShapeDtypeStruct((B, S, ATT_WIDTH), F32),
        compiler_params=cparams(2),
        name="dilated_attn_d1",
    )(nat, nat, nat, nat, nat, o1, l1)


def _split3(t):
    hi = t.astype(BF16)
    r1 = t - hi.astype(F32)
    mid = r1.astype(BF16)
    lo = (r1 - mid.astype(F32)).astype(BF16)
    return hi, mid, lo


def _outproj_router_kernel(attn_ref, conv_ref, x_ref, ga_ref, wo_ref, g2_ref, wrt_ref, br_ref,
                           h_ref, hn_ref, idx_ref, gate_ref, rank_ref, cnt_ref, carry_ref):
    tm = x_ref.shape[0]
    step = pl.program_id(0)

    @pl.when(step == 0)
    def _():
        carry_ref[...] = jnp.zeros_like(carry_ref)

    attn_n = _rms(attn_ref[...], ga_ref[...]).astype(BF16)
    mixed = jnp.dot(attn_n, wo_ref[0:ATT_WIDTH, :], preferred_element_type=F32)
    mixed += jnp.dot(conv_ref[...], wo_ref[ATT_WIDTH:, :], preferred_element_type=F32)
    h = x_ref[...] + mixed
    h_ref[...] = h
    hn = _rms(h, g2_ref[...])
    hn_ref[...] = hn

    a0, a1, a2 = _split3(hn)
    dot_t = lambda w, a: lax.dot_general(w, a, (((1,), (1,)), ((), ())),
                                         preferred_element_type=F32)
    ne = N_EXPERTS
    p0 = dot_t(wrt_ref[...], a0)
    p1 = dot_t(wrt_ref[0:2 * ne, :], a1)
    p2 = dot_t(wrt_ref[0:ne, :], a2)
    logits = (p0[0:ne] + (p0[ne:2 * ne] + p1[0:ne])
              + (p0[2 * ne:] + p1[ne:] + p2)) + br_ref[...]

    eidx = lax.broadcasted_iota(jnp.int32, (N_EXPERTS, tm), 0)
    work = logits
    vals, idxs = [], []
    for _ in range(TOP_K):
        mval = jnp.max(work, axis=0, keepdims=True)
        midx = jnp.min(jnp.where(work == mval, eidx, N_EXPERTS), axis=0, keepdims=True)
        vals.append(mval)
        idxs.append(midx)
        work = jnp.where(eidx == midx, -jnp.inf, work)
    exps = [jnp.exp(v - vals[0]) for v in vals]
    denom = exps[0] + exps[1] + exps[2] + exps[3]

    chosen = jnp.zeros((N_EXPERTS, tm), F32)
    for midx in idxs:
        chosen = chosen + (eidx == midx).astype(F32)
    si = lax.broadcasted_iota(jnp.int32, (tm, tm), 0)
    ti = lax.broadcasted_iota(jnp.int32, (tm, tm), 1)
    earlier = (si < ti).astype(BF16)
    before = jnp.dot(chosen.astype(BF16), earlier, preferred_element_type=F32) + carry_ref[...]
    for kk in range(TOP_K):
        idx_ref[kk:kk + 1, :] = idxs[kk]
        gate_ref[kk:kk + 1, :] = exps[kk] / denom
        rank_ref[kk:kk + 1, :] = jnp.sum(
            jnp.where(eidx == idxs[kk], before, 0.0), axis=0, keepdims=True).astype(jnp.int32)
    carry_ref[...] = carry_ref[...] + jnp.sum(chosen, axis=1, keepdims=True)
    cnt_ref[...] = carry_ref[...].astype(jnp.int32)


def _out_projection_router(attn, conv_n, x, attn_g, w_out, norm2_g, w_router, b_router):
    T, D = x.shape
    tm = PROJ_TILE
    tok = lambda width: pl.BlockSpec((tm, width), lambda i: (i, 0))
    per_k = pl.BlockSpec((TOP_K, tm), lambda i: (0, i))
    full = lambda shape: pl.BlockSpec(shape, lambda i: (0,) * len(shape))
    wrt_pieces = jnp.concatenate(_split3(w_router.T), axis=0)
    return pl.pallas_call(
        _outproj_router_kernel,
        grid=(T // tm,),
        in_specs=[tok(ATT_WIDTH), tok(CONV_WIDTH), tok(D), full((1, ATT_WIDTH)),
                  full((ATT_WIDTH + CONV_WIDTH, D)), full((1, D)),
                  full((3 * N_EXPERTS, D)), full((N_EXPERTS, 1))],
        out_specs=[tok(D), tok(D), per_k, per_k, per_k, full((N_EXPERTS, 1))],
        out_shape=[jax.ShapeDtypeStruct((T, D), F32), jax.ShapeDtypeStruct((T, D), F32),
                   jax.ShapeDtypeStruct((TOP_K, T), jnp.int32),
                   jax.ShapeDtypeStruct((TOP_K, T), F32),
                   jax.ShapeDtypeStruct((TOP_K, T), jnp.int32),
                   jax.ShapeDtypeStruct((N_EXPERTS, 1), jnp.int32)],
        scratch_shapes=[pltpu.VMEM((N_EXPERTS, 1), F32)],
        compiler_params=pltpu.CompilerParams(
            dimension_semantics=("arbitrary",), vmem_limit_bytes=VMEM_LIMIT),
        name="outproj_router",
    )(attn, conv_n, x, attn_g.reshape(1, -1), w_out, norm2_g.reshape(1, D),
      wrt_pieces, b_router.reshape(N_EXPERTS, 1))


def _dispatch_kernel(dest_ref, hn_ref, xb_ref, sem):
    tm = hn_ref.shape[0]

    for t in range(tm):
        for kk in range(TOP_K):
            pltpu.make_async_copy(
                hn_ref.at[pl.ds(t, 1)], xb_ref.at[pl.ds(dest_ref[kk, t], 1)], sem
            ).start(priority=kk % 2)
    for _ in range(TOP_K):
        pltpu.make_async_copy(hn_ref, xb_ref.at[pl.ds(0, tm)], sem).wait()


def _dispatch(hn, dest, n_rows):
    T, D = hn.shape
    tm = DISPATCH_TILE
    return pl.pallas_call(
        _dispatch_kernel,
        grid=(T // tm,),
        in_specs=[pl.BlockSpec((TOP_K, tm), lambda i: (0, i), memory_space=pltpu.SMEM),
                  pl.BlockSpec((tm, D), lambda i: (i, 0))],
        out_specs=pl.BlockSpec(memory_space=pl.ANY),
        out_shape=jax.ShapeDtypeStruct((n_rows, D), F32),
        scratch_shapes=[pltpu.SemaphoreType.DMA(())],
        compiler_params=pltpu.CompilerParams(
            dimension_semantics=("arbitrary",), has_side_effects=True),
        name="moe_dispatch",
    )(dest, hn)


def _expert_kernel(vblk_ref, vexp_ref, vnew_ref, nvis_ref,
                   x_ref, wgu_ref, bgu_ref, wdn_ref, bdn_ref, y_ref, wgu_bf_ref, wdn_bf_ref):
    ff = wdn_ref.shape[1]
    vis = pl.program_id(0)

    @pl.when(vis < nvis_ref[0])
    def _():
        @pl.when(vnew_ref[vis] == 1)
        def _():
            wgu_bf_ref[...] = wgu_ref[0].astype(BF16)
            wdn_bf_ref[...] = wdn_ref[0].astype(BF16)

        xb = x_ref[...].astype(BF16)
        gu = jnp.dot(xb, wgu_bf_ref[...], preferred_element_type=F32) + bgu_ref[0]
        g = jnp.minimum(gu[:, :ff], SWIGLU_LIMIT)
        up = jnp.clip(gu[:, ff:], -SWIGLU_LIMIT, SWIGLU_LIMIT)
        act = (up + 1.0) * (g * jax.nn.sigmoid(SWIGLU_ALPHA * g))
        y_ref[...] = (jnp.dot(act.astype(BF16), wdn_bf_ref[...], preferred_element_type=F32)
                      + bdn_ref[0])

    @pl.when(vis >= nvis_ref[0])
    def _():
        y_ref[...] = jnp.zeros_like(y_ref)


def _experts(xb, visits, w_gu, b_gu, w_dn, b_dn):
    M, D = xb.shape
    E, _, ff2 = w_gu.shape
    bm = EXPERT_BLOCK
    vblk, vexp, vnew, nvis = visits
    row_map = lambda v, vb, ve, vn, nv: (vb[v], 0)
    exp_map = lambda v, vb, ve, vn, nv: (ve[v], 0, 0)
    grid_spec = pltpu.PrefetchScalarGridSpec(
        num_scalar_prefetch=4,
        grid=(vblk.shape[0],),
        in_specs=[
            pl.BlockSpec((bm, D), row_map),
            pl.BlockSpec((1, D, ff2), exp_map),
            pl.BlockSpec((1, 1, ff2), exp_map),
            pl.BlockSpec((1, ff2 // 2, D), exp_map),
            pl.BlockSpec((1, 1, D), exp_map),
        ],
        out_specs=pl.BlockSpec((bm, D), lambda v, vb, ve, vn, nv: (v, 0)),
        scratch_shapes=[pltpu.VMEM((D, ff2), BF16), pltpu.VMEM((ff2 // 2, D), BF16)],
    )
    return pl.pallas_call(
        _expert_kernel,
        grid_spec=grid_spec,
        out_shape=jax.ShapeDtypeStruct((vblk.shape[0] * bm, D), F32),
        compiler_params=pltpu.CompilerParams(
            dimension_semantics=("arbitrary",), vmem_limit_bytes=VMEM_LIMIT),
        name="moe_experts",
    )(vblk, vexp, vnew, nvis, xb, w_gu, b_gu.reshape(E, 1, ff2), w_dn,
      b_dn.reshape(E, 1, D))


def _visit_schedule(counts, n_rows, bm):
    n_blocks = n_rows // bm
    max_visits = n_blocks + N_EXPERTS - 1
    ends = jnp.cumsum(counts)
    starts = ends - counts
    first_blk = starts // bm
    last_blk = jnp.maximum(ends - 1, 0) // bm
    n_vis = jnp.where(counts > 0, last_blk - first_blk + 1, 0)
    vis_end = jnp.cumsum(n_vis)
    vis_start = vis_end - n_vis
    total = vis_end[-1]
    v = jnp.minimum(jnp.arange(max_visits, dtype=jnp.int32), total - 1)
    vexp = jnp.sum(v[:, None] >= vis_end[None, :], axis=-1).astype(jnp.int32)
    pick = lambda table: jnp.sum(
        jnp.where(vexp[:, None] == jnp.arange(N_EXPERTS, dtype=jnp.int32), table[None, :], 0), axis=-1)
    vblk = (pick(first_blk) + v - pick(vis_start)).astype(jnp.int32)
    vnew = jnp.concatenate([jnp.ones((1,), jnp.int32),
                            (vexp[1:] != vexp[:-1]).astype(jnp.int32)])
    tables = (vblk, vexp, vnew, total.astype(jnp.int32).reshape(1))
    return tables, (vis_start - first_blk).astype(jnp.int32)


def _combine_kernel(dest_ref, dest_next_ref, yb_ref, gate_ref, h_ref, gf_ref, out_ref,
                    buf_ref, sem):
    tm = h_ref.shape[0]
    step = pl.program_id(0)
    last = pl.num_programs(0) - 1

    def issue(idx_ref, slot):
        for t in range(tm):
            for kk in range(TOP_K):
                pltpu.make_async_copy(
                    yb_ref.at[pl.ds(idx_ref[kk, t], 1)],
                    buf_ref.at[slot, kk, pl.ds(t, 1)], sem.at[slot]
                ).start(priority=kk % 2)

    def reduce(slot):
        for kk in range(TOP_K):
            pltpu.make_async_copy(yb_ref.at[pl.ds(0, tm)], buf_ref.at[slot, kk], sem.at[slot]).wait()
        gates = gate_ref[...]
        y = h_ref[...]
        for kk in range(TOP_K):
            y = y + gates[:, kk:kk + 1] * buf_ref[slot, kk]
        out_ref[...] = _rms(y, gf_ref[...])

    @pl.when(step == 0)
    def _():
        issue(dest_ref, 0)

    for slot in range(2):
        @pl.when(step % 2 == slot)
        def _():
            @pl.when(step < last)
            def _():
                issue(dest_next_ref, 1 - slot)
            reduce(slot)


def _combine(yb, src, gates, h, final_g):
    T, D = h.shape
    tm = COMBINE_TILE
    n_steps = T // tm
    return pl.pallas_call(
        _combine_kernel,
        grid=(n_steps,),
        in_specs=[pl.BlockSpec((TOP_K, tm), lambda i: (0, i), memory_space=pltpu.SMEM),
                  pl.BlockSpec((TOP_K, tm), lambda i: (0, jnp.minimum(i + 1, n_steps - 1)),
                               memory_space=pltpu.SMEM),
                  pl.BlockSpec(memory_space=pl.ANY),
                  pl.BlockSpec((tm, TOP_K), lambda i: (i, 0)),
                  pl.BlockSpec((tm, D), lambda i: (i, 0)),
                  pl.BlockSpec((1, D), lambda i: (0, 0))],
        out_specs=pl.BlockSpec((tm, D), lambda i: (i, 0)),
        out_shape=jax.ShapeDtypeStruct((T, D), F32),
        scratch_shapes=[pltpu.VMEM((2, TOP_K, tm, D), F32), pltpu.SemaphoreType.DMA((2,))],
        compiler_params=pltpu.CompilerParams(
            dimension_semantics=("arbitrary",), vmem_limit_bytes=VMEM_LIMIT),
        name="moe_combine",
    )(src, src, yb, gates, h, final_g.reshape(1, D))


def _rope_tables(seq_len):
    half = HEAD_DIM // 2
    inv_freq = 1.0 / (ROPE_THETA ** (jnp.arange(0, HEAD_DIM, 2, dtype=F32) / HEAD_DIM))
    ang = jnp.arange(seq_len, dtype=F32)[:, None] * inv_freq[None, :]
    cos = jnp.tile(jnp.cos(ang), (1, LANES // half))
    sin = jnp.tile(jnp.concatenate([-jnp.sin(ang), jnp.sin(ang)], axis=-1), (1, LANES // HEAD_DIM))
    return cos, sin


def _moe(hn, h, idx, gates, rank, counts, w_gu, b_gu, w_dn, b_dn, final_g):
    T, D = hn.shape
    n_assign = T * TOP_K
    counts = counts.reshape(N_EXPERTS)
    starts = jnp.cumsum(counts) - counts
    onehot = idx[..., None] == jnp.arange(N_EXPERTS, dtype=jnp.int32)
    dest = rank + jnp.sum(jnp.where(onehot, starts, 0), axis=-1)
    bm = EXPERT_BLOCK
    visits, visit_offset = _visit_schedule(counts, n_assign, bm)
    visit = dest // bm + jnp.sum(jnp.where(onehot, visit_offset, 0), axis=-1)
    src = visit * bm + dest % bm

    xb = _dispatch(hn, dest.astype(jnp.int32), n_assign)
    yb = _experts(xb, visits, w_gu, b_gu, w_dn, b_dn)
    return _combine(yb, src.astype(jnp.int32), gates.T, h, final_g)


def kernel(x, norm1_g, w_mix_in, attn_norm_g, conv_w, conv_norm_g, w_mix_out, norm2_g,
           w_router, b_router, w_gate_up, b_gate_up, w_down, b_down, final_norm_g):
    B, S, D = x.shape
    assert norm1_g.shape[0] == 1, "single-layer block"
    cos_tab, sin_tab = _rope_tables(S)
    w_in = w_mix_in[0].astype(BF16)
    w_in = jnp.concatenate([w_in[:, 3 * ATT_WIDTH:], w_in[:, :3 * ATT_WIDTH]], axis=1)
    nat, d4, d16, conv_n = _in_projection(x, norm1_g[0], w_in, conv_w[0],
                                          conv_norm_g[0], cos_tab, sin_tab)
    attn = _dilated_attention(nat, d4, d16)

    T = B * S
    h, hn, idx, gates, rank, counts = _out_projection_router(
        attn.reshape(T, ATT_WIDTH), conv_n.reshape(T, CONV_WIDTH), x.reshape(T, D),
        attn_norm_g[0], w_mix_out[0].astype(BF16), norm2_g[0], w_router[0], b_router[0])
    out = _moe(hn, h, idx, gates, rank, counts, w_gate_up[0], b_gate_up[0], w_down[0],
               b_down[0], final_norm_g)
    return out.reshape(B, S, D)
```

```python
import functools

import jax
import jax.numpy as jnp
from jax import lax
from jax.experimental import pallas as pl
from jax.experimental.pallas import tpu as pltpu

F32 = jnp.float32
BF16 = jnp.bfloat16

N_HEADS = 8
HEAD_DIM = 64
ATT_WIDTH = N_HEADS * HEAD_DIM
CONV_WIDTH = 512
ATT_BLOCK = 128
ROPE_THETA = 10000.0
N_EXPERTS = 32
TOP_K = 4
SWIGLU_ALPHA = 1.702
SWIGLU_LIMIT = 7.0
NORM_EPS = 1e-5
NEG_BIG = -1e30

LANES = 128
N_PAIRS = ATT_WIDTH // LANES
VMEM_LIMIT = 52 * 1024 * 1024

PROJ_TILE = 512
ATT_ROWS = 2048
ATT_CLASS_ROWS = 512
EXPERT_BLOCK = 512
DISPATCH_TILE = 512
COMBINE_TILE = 256


def _rms(x, g):
    return x * lax.rsqrt(jnp.mean(x * x, axis=-1, keepdims=True) + NORM_EPS) * g


def _inproj_kernel(x_ref, g1_ref, w_ref, cos_ref, sin_ref, convw_ref, convg_ref,
                   nat_ref, d4_ref, d16_ref, c_ref, zbuf_ref, sa_ref, sb_ref):
    tm = x_ref.shape[1]

    @pl.when(pl.program_id(1) == 0)
    def _():
        zbuf_ref[tm:tm + 8, :] = jnp.zeros((8, CONV_WIDTH), F32)

    hn = _rms(x_ref[0], g1_ref[...]).astype(BF16)
    proj = jnp.dot(hn, w_ref[...], preferred_element_type=F32)

    cos = cos_ref[...]
    sin = sin_ref[...]
    lane = lax.broadcasted_iota(jnp.int32, (tm, LANES), 1)
    first_half = (lane % HEAD_DIM) < (HEAD_DIM // 2)

    def rope(t):
        partner = jnp.where(first_half, pltpu.roll(t, LANES - HEAD_DIM // 2, 1),
                            pltpu.roll(t, HEAD_DIM // 2, 1))
        return t * cos + partner * sin

    for ti in range(3):
        for c in range(N_PAIRS):
            col = 3 * CONV_WIDTH + ti * ATT_WIDTH + c * LANES
            val = proj[:, col:col + LANES]
            if ti == 0:
                val = rope(val) * (HEAD_DIM ** -0.5)
            elif ti == 1:
                val = rope(val)
            nat_ref[ti, c] = val.astype(BF16)
            slab = ti * N_PAIRS + c
            sa_ref[slab] = val
            for r in range(4):
                piece = sa_ref[slab, pl.ds(r, tm // 4, stride=4), :]
                d4_ref[ti, c, r] = piece.astype(BF16)
                sb_ref[slab, r] = piece
            for r4 in range(4):
                for j in range(4):
                    piece = sb_ref[slab, r4, pl.ds(j, tm // 16, stride=4), :]
                    d16_ref[ti, c, r4 + 4 * j] = piece.astype(BF16)

    u = proj[:, 0:CONV_WIDTH]
    gate_b = proj[:, CONV_WIDTH:2 * CONV_WIDTH]
    gate_c = proj[:, 2 * CONV_WIDTH:3 * CONV_WIDTH]
    z = gate_c * u

    zbuf_ref[0:8, :] = zbuf_ref[tm:tm + 8, :]
    zbuf_ref[8:8 + tm, :] = z
    z1 = zbuf_ref[7:7 + tm, :]
    z2 = zbuf_ref[6:6 + tm, :]
    conv = z2 * convw_ref[0:1, :] + z1 * convw_ref[1:2, :] + z * convw_ref[2:3, :]
    c_ref[0] = _rms(gate_b * conv, convg_ref[...]).astype(BF16)


def _in_projection(x, norm1_g, w_in, conv_w, conv_g, cos_tab, sin_tab):
    B, S, D = x.shape
    tm = PROJ_TILE
    width = w_in.shape[1]
    full = lambda shape: pl.BlockSpec(shape, lambda b, i: (0,) * len(shape))
    nat = jax.ShapeDtypeStruct((3, N_PAIRS, B, S, LANES), BF16)
    d4 = jax.ShapeDtypeStruct((3, N_PAIRS, B, 4, S // 4, LANES), BF16)
    d16 = jax.ShapeDtypeStruct((3, N_PAIRS, B, 16, S // 16, LANES), BF16)
    return pl.pallas_call(
        _inproj_kernel,
        grid=(B, S // tm),
        in_specs=[
            pl.BlockSpec((1, tm, D), lambda b, i: (b, i, 0)),
            full((1, D)),
            full((D, width)),
            pl.BlockSpec((tm, LANES), lambda b, i: (i, 0)),
            pl.BlockSpec((tm, LANES), lambda b, i: (i, 0)),
            full((3, CONV_WIDTH)),
            full((1, CONV_WIDTH)),
        ],
        out_specs=[
            pl.BlockSpec((3, N_PAIRS, None, tm, LANES), lambda b, i: (0, 0, b, i, 0)),
            pl.BlockSpec((3, N_PAIRS, None, 4, tm // 4, LANES), lambda b, i: (0, 0, b, 0, i, 0)),
            pl.BlockSpec((3, N_PAIRS, None, 16, tm // 16, LANES), lambda b, i: (0, 0, b, 0, i, 0)),
            pl.BlockSpec((1, tm, CONV_WIDTH), lambda b, i: (b, i, 0)),
        ],
        out_shape=[nat, d4, d16, jax.ShapeDtypeStruct((B, S, CONV_WIDTH), BF16)],
        scratch_shapes=[pltpu.VMEM((tm + 8, CONV_WIDTH), F32),
                        pltpu.VMEM((3 * N_PAIRS, tm, LANES), F32),
                        pltpu.VMEM((3 * N_PAIRS, 4, tm // 4, LANES), F32)],
        compiler_params=pltpu.CompilerParams(
            dimension_semantics=("arbitrary", "arbitrary"), vmem_limit_bytes=VMEM_LIMIT),
        name="inproj",
    )(x, norm1_g.reshape(1, D), w_in, cos_tab, sin_tab, conv_w, conv_g.reshape(1, CONV_WIDTH))


def _band_bias(is_first):
    blk = ATT_BLOCK
    qi = lax.broadcasted_iota(jnp.int32, (blk, 2 * blk), 0)
    kj = lax.broadcasted_iota(jnp.int32, (blk, 2 * blk), 1)
    band = (kj >= qi) & (kj <= qi + blk)
    if is_first is not None:
        band = band & ((kj >= blk) | jnp.logical_not(is_first))
    return jnp.where(band, 0.0, NEG_BIG).astype(F32)


def _pair_attention(q2, k2, v2, bias, head0):
    parts = []
    for hh in range(2):
        sel = head0 if hh == 0 else jnp.logical_not(head0)
        qm = jnp.where(sel, q2, jnp.zeros_like(q2))
        s = lax.dot_general(qm, k2, (((1,), (1,)), ((), ())), preferred_element_type=F32) + bias
        m = jnp.max(s, axis=-1, keepdims=True)
        p = jnp.exp(s - m)
        l = jnp.sum(p, axis=-1, keepdims=True)
        parts.append((m, l, jnp.dot(p.astype(BF16), v2, preferred_element_type=F32)))
    (m0, l0, o0), (m1, l1, o1) = parts
    shape = o0.shape
    pick = lambda a, b: jnp.where(head0, jnp.broadcast_to(a, shape), jnp.broadcast_to(b, shape))
    return pick(m0, m1), pick(l0, l1), jnp.where(head0, o0, o1)


def _finish(m, l, o_un):
    return o_un * (1.0 / l), m + jnp.log(l)


def _merge(o_old, lse_old, m, l, o_un, want_lse=True):
    mx = jnp.maximum(lse_old, m)
    wa = jnp.exp(lse_old - mx)
    eb = jnp.exp(m - mx)
    tot = wa + eb * l
    o = (wa * o_old + eb * o_un) * (1.0 / tot)
    return o, (mx + jnp.log(tot) if want_lse else None)


def _attn_classes_kernel(*refs, has_prev):
    if has_prev:
        q_ref, kp_ref, kc_ref, vp_ref, vc_ref, op_ref, lp_ref, o_ref, lse_ref = refs
    else:
        q_ref, kp_ref, kc_ref, vp_ref, vc_ref, o_ref, lse_ref = refs
    blk = ATT_BLOCK
    rows_total = q_ref.shape[2]
    bias_band = _band_bias(None)
    bias_first = _band_bias(pl.program_id(1 if has_prev else 2) == 0)
    head0 = lax.broadcasted_iota(jnp.int32, (blk, LANES), 1) < HEAD_DIM
    for cls in range(4):
        for j in range(rows_total // blk):
            rows = slice(j * blk, (j + 1) * blk)
            for c in range(N_PAIRS):
                if j == 0:
                    k2 = jnp.concatenate([kp_ref[c, cls], kc_ref[c, cls, 0:blk]], axis=0)
                    v2 = jnp.concatenate([vp_ref[c, cls], vc_ref[c, cls, 0:blk]], axis=0)
                else:
                    k2 = kc_ref[c, cls, (j - 1) * blk:(j + 1) * blk]
                    v2 = vc_ref[c, cls, (j - 1) * blk:(j + 1) * blk]
                m, l, o_un = _pair_attention(q_ref[c, cls, rows], k2, v2,
                                             bias_first if j == 0 else bias_band, head0)
                if has_prev:
                    o_pair, lse_pair = _merge(op_ref[c, cls, rows], lp_ref[c, cls, rows], m, l, o_un)
                else:
                    o_pair, lse_pair = _finish(m, l, o_un)
                out_rows = pl.ds(cls + 4 * blk * j, blk, stride=4)
                o_ref[c, out_rows, :] = o_pair
                lse_ref[c, out_rows, :] = lse_pair


def _attn_natural_kernel(q_ref, kp_ref, kc_ref, vp_ref, vc_ref, op_ref, lp_ref, o_ref):
    blk = ATT_BLOCK
    rows_total = q_ref.shape[1]
    bias_band = _band_bias(None)
    bias_first = _band_bias(pl.program_id(1) == 0)
    head0 = lax.broadcasted_iota(jnp.int32, (blk, LANES), 1) < HEAD_DIM
    for j in range(rows_total // blk):
        rows = slice(j * blk, (j + 1) * blk)
        for c in range(N_PAIRS):
            if j == 0:
                k2 = jnp.concatenate([kp_ref[c], kc_ref[c, 0:blk]], axis=0)
                v2 = jnp.concatenate([vp_ref[c], vc_ref[c, 0:blk]], axis=0)
            else:
                k2 = kc_ref[c, (j - 1) * blk:(j + 1) * blk]
                v2 = vc_ref[c, (j - 1) * blk:(j + 1) * blk]
            m, l, o_un = _pair_attention(q_ref[c, rows], k2, v2,
                                         bias_first if j == 0 else bias_band, head0)
            o_pair, _ = _merge(op_ref[c, rows], lp_ref[c, rows], m, l, o_un, want_lse=False)
            o_ref[rows, c * LANES:(c + 1) * LANES] = o_pair


def _dilated_attention(nat, d4, d16):
    _, _, B, S, _ = nat.shape
    blk = ATT_BLOCK
    cparams = lambda n: pltpu.CompilerParams(
        dimension_semantics=("arbitrary",) * n, vmem_limit_bytes=VMEM_LIMIT)

    L16, L4 = S // 16, S // 4
    d16v = d16.reshape(3, N_PAIRS, B, 4, 4, L16, LANES)
    crows = ATT_CLASS_ROWS
    cper = crows // blk
    assert L16 % crows == 0 and S % ATT_ROWS == 0, "sequence too short for the attention tiling"
    def spec16(t, prev):
        if prev:
            return pl.BlockSpec((None, N_PAIRS, None, 4, None, blk, LANES),
                                lambda b, r4, n: (t, 0, b, 0, r4, jnp.maximum(n * cper - 1, 0), 0))
        return pl.BlockSpec((None, N_PAIRS, None, 4, None, crows, LANES),
                            lambda b, r4, n: (t, 0, b, 0, r4, n, 0))
    state4 = jax.ShapeDtypeStruct((N_PAIRS, B, 4, L4, LANES), F32)
    out16 = pl.BlockSpec((N_PAIRS, None, None, 4 * crows, LANES), lambda b, r4, n: (0, b, r4, n, 0))
    o4, l4 = pl.pallas_call(
        functools.partial(_attn_classes_kernel, has_prev=False),
        grid=(B, 4, L16 // crows),
        in_specs=[spec16(0, False), spec16(1, True), spec16(1, False), spec16(2, True),
                  spec16(2, False)],
        out_specs=[out16, out16],
        out_shape=[state4, state4],
        compiler_params=cparams(3),
        name="dilated_attn_d16",
    )(d16v, d16v, d16v, d16v, d16v)

    def spec4(t, prev):
        if prev:
            return pl.BlockSpec((None, N_PAIRS, None, 4, blk, LANES),
                                lambda b, n: (t, 0, b, 0, jnp.maximum(n * cper - 1, 0), 0))
        return pl.BlockSpec((None, N_PAIRS, None, 4, crows, LANES), lambda b, n: (t, 0, b, 0, n, 0))
    in_state4 = pl.BlockSpec((N_PAIRS, None, 4, crows, LANES), lambda b, n: (0, b, 0, n, 0))
    state1 = jax.ShapeDtypeStruct((N_PAIRS, B, S, LANES), F32)
    out4 = pl.BlockSpec((N_PAIRS, None, 4 * crows, LANES), lambda b, n: (0, b, n, 0))
    o1, l1 = pl.pallas_call(
        functools.partial(_attn_classes_kernel, has_prev=True),
        grid=(B, L4 // crows),
        in_specs=[spec4(0, False), spec4(1, True), spec4(1, False), spec4(2, True),
                  spec4(2, False), in_state4, in_state4],
        out_specs=[out4, out4],
        out_shape=[state1, state1],
        compiler_params=cparams(2),
        name="dilated_attn_d4",
    )(d4, d4, d4, d4, d4, o4, l4)

    rows = ATT_ROWS
    per = rows // blk
    cur1 = lambda t: pl.BlockSpec((None, N_PAIRS, None, rows, LANES), lambda b, n: (t, 0, b, n, 0))
    prev1 = lambda t: pl.BlockSpec((None, N_PAIRS, None, blk, LANES),
                                   lambda b, n: (t, 0, b, jnp.maximum(n * per - 1, 0), 0))
    in_state1 = pl.BlockSpec((N_PAIRS, None, rows, LANES), lambda b, n: (0, b, n, 0))
    return pl.pallas_call(
        _attn_natural_kernel,
        grid=(B, S // rows),
        in_specs=[cur1(0), prev1(1), cur1(1), prev1(2), cur1(2), in_state1, in_state1],
        out_specs=pl.BlockSpec((None, rows, ATT_WIDTH), lambda b, n: (b, n, 0)),
        out_shape=jax.ShapeDtypeStruct((B, S, ATT_WIDTH), F32),
        compiler_params=cparams(2),
        name="dilated_attn_d1",
    )(nat, nat, nat, nat, nat, o1, l1)


def _split3(t):
    hi = t.astype(BF16)
    r1 = t - hi.astype(F32)
    mid = r1.astype(BF16)
    lo = (r1 - mid.astype(F32)).astype(BF16)
    return hi, mid, lo


def _outproj_router_kernel(attn_ref, conv_ref, x_ref, ga_ref, wo_ref, g2_ref, wrt_ref, br_ref,
                           h_ref, hn_ref, idx_ref, gate_ref, rank_ref, cnt_ref, carry_ref):
    tm = x_ref.shape[0]
    step = pl.program_id(0)

    @pl.when(step == 0)
    def _():
        carry_ref[...] = jnp.zeros_like(carry_ref)

    attn_n = _rms(attn_ref[...], ga_ref[...]).astype(BF16)
    mixed = jnp.dot(attn_n, wo_ref[0:ATT_WIDTH, :], preferred_element_type=F32)
    mixed += jnp.dot(conv_ref[...], wo_ref[ATT_WIDTH:, :], preferred_element_type=F32)
    h = x_ref[...] + mixed
    h_ref[...] = h
    hn = _rms(h, g2_ref[...])
    hn_ref[...] = hn

    a0, a1, a2 = _split3(hn)
    dot_t = lambda w, a: lax.dot_general(w, a, (((1,), (1,)), ((), ())),
                                         preferred_element_type=F32)
    ne = N_EXPERTS
    p0 = dot_t(wrt_ref[...], a0)
    p1 = dot_t(wrt_ref[0:2 * ne, :], a1)
    p2 = dot_t(wrt_ref[0:ne, :], a2)
    logits = (p0[0:ne] + (p0[ne:2 * ne] + p1[0:ne])
              + (p0[2 * ne:] + p1[ne:] + p2)) + br_ref[...]

    eidx = lax.broadcasted_iota(jnp.int32, (N_EXPERTS, tm), 0)
    work = logits
    vals, idxs = [], []
    for _ in range(TOP_K):
        mval = jnp.max(work, axis=0, keepdims=True)
        midx = jnp.min(jnp.where(work == mval, eidx, N_EXPERTS), axis=0, keepdims=True)
        vals.append(mval)
        idxs.append(midx)
        work = jnp.where(eidx == midx, -jnp.inf, work)
    exps = [jnp.exp(v - vals[0]) for v in vals]
    denom = exps[0] + exps[1] + exps[2] + exps[3]

    chosen = jnp.zeros((N_EXPERTS, tm), F32)
    for midx in idxs:
        chosen = chosen + (eidx == midx).astype(F32)
    si = lax.broadcasted_iota(jnp.int32, (tm, tm), 0)
    ti = lax.broadcasted_iota(jnp.int32, (tm, tm), 1)
    earlier = (si < ti).astype(BF16)
    before = jnp.dot(chosen.astype(BF16), earlier, preferred_element_type=F32) + carry_ref[...]
    for kk in range(TOP_K):
        idx_ref[kk:kk + 1, :] = idxs[kk]
        gate_ref[kk:kk + 1, :] = exps[kk] / denom
        rank_ref[kk:kk + 1, :] = jnp.sum(
            jnp.where(eidx == idxs[kk], before, 0.0), axis=0, keepdims=True).astype(jnp.int32)
    carry_ref[...] = carry_ref[...] + jnp.sum(chosen, axis=1, keepdims=True)
    cnt_ref[...] = carry_ref[...].astype(jnp.int32)


def _out_projection_router(attn, conv_n, x, attn_g, w_out, norm2_g, w_router, b_router):
    T, D = x.shape
    tm = PROJ_TILE
    tok = lambda width: pl.BlockSpec((tm, width), lambda i: (i, 0))
    per_k = pl.BlockSpec((TOP_K, tm), lambda i: (0, i))
    full = lambda shape: pl.BlockSpec(shape, lambda i: (0,) * len(shape))
    wrt_pieces = jnp.concatenate(_split3(w_router.T), axis=0)
    return pl.pallas_call(
        _outproj_router_kernel,
        grid=(T // tm,),
        in_specs=[tok(ATT_WIDTH), tok(CONV_WIDTH), tok(D), full((1, ATT_WIDTH)),
                  full((ATT_WIDTH + CONV_WIDTH, D)), full((1, D)),
                  full((3 * N_EXPERTS, D)), full((N_EXPERTS, 1))],
        out_specs=[tok(D), tok(D), per_k, per_k, per_k, full((N_EXPERTS, 1))],
        out_shape=[jax.ShapeDtypeStruct((T, D), F32), jax.ShapeDtypeStruct((T, D), F32),
                   jax.ShapeDtypeStruct((TOP_K, T), jnp.int32),
                   jax.ShapeDtypeStruct((TOP_K, T), F32),
                   jax.ShapeDtypeStruct((TOP_K, T), jnp.int32),
                   jax.ShapeDtypeStruct((N_EXPERTS, 1), jnp.int32)],
        scratch_shapes=[pltpu.VMEM((N_EXPERTS, 1), F32)],
        compiler_params=pltpu.CompilerParams(
            dimension_semantics=("arbitrary",), vmem_limit_bytes=VMEM_LIMIT),
        name="outproj_router",
    )(attn, conv_n, x, attn_g.reshape(1, -1), w_out, norm2_g.reshape(1, D),
      wrt_pieces, b_router.reshape(N_EXPERTS, 1))


def _dispatch_kernel(dest_ref, hn_ref, xb_ref, sem):
    tm = hn_ref.shape[0]

    for t in range(tm):
        for kk in range(TOP_K):
            pltpu.make_async_copy(
                hn_ref.at[pl.ds(t, 1)], xb_ref.at[pl.ds(dest_ref[kk, t], 1)], sem
            ).start(priority=kk % 2)
    for _ in range(TOP_K):
        pltpu.make_async_copy(hn_ref, xb_ref.at[pl.ds(0, tm)], sem).wait()


def _dispatch(hn, dest, n_rows):
    T, D = hn.shape
    tm = DISPATCH_TILE
    return pl.pallas_call(
        _dispatch_kernel,
        grid=(T // tm,),
        in_specs=[pl.BlockSpec((TOP_K, tm), lambda i: (0, i), memory_space=pltpu.SMEM),
                  pl.BlockSpec((tm, D), lambda i: (i, 0))],
        out_specs=pl.BlockSpec(memory_space=pl.ANY),
        out_shape=jax.ShapeDtypeStruct((n_rows, D), F32),
        scratch_shapes=[pltpu.SemaphoreType.DMA(())],
        compiler_params=pltpu.CompilerParams(
            dimension_semantics=("arbitrary",), has_side_effects=True),
        name="moe_dispatch",
    )(dest, hn)


def _expert_kernel(vblk_ref, vexp_ref, vnew_ref, nvis_ref,
                   x_ref, wgu_ref, bgu_ref, wdn_ref, bdn_ref, y_ref, wgu_bf_ref, wdn_bf_ref):
    ff = wdn_ref.shape[1]
    vis = pl.program_id(0)

    @pl.when(vis < nvis_ref[0])
    def _():
        @pl.when(vnew_ref[vis] == 1)
        def _():
            wgu_bf_ref[...] = wgu_ref[0].astype(BF16)
            wdn_bf_ref[...] = wdn_ref[0].astype(BF16)

        xb = x_ref[...].astype(BF16)
        gu = jnp.dot(xb, wgu_bf_ref[...], preferred_element_type=F32) + bgu_ref[0]
        g = jnp.minimum(gu[:, :ff], SWIGLU_LIMIT)
        up = jnp.clip(gu[:, ff:], -SWIGLU_LIMIT, SWIGLU_LIMIT)
        act = (up + 1.0) * (g * jax.nn.sigmoid(SWIGLU_ALPHA * g))
        y_ref[...] = (jnp.dot(act.astype(BF16), wdn_bf_ref[...], preferred_element_type=F32)
                      + bdn_ref[0])

    @pl.when(vis >= nvis_ref[0])
    def _():
        y_ref[...] = jnp.zeros_like(y_ref)


def _experts(xb, visits, w_gu, b_gu, w_dn, b_dn):
    M, D = xb.shape
    E, _, ff2 = w_gu.shape
    bm = EXPERT_BLOCK
    vblk, vexp, vnew, nvis = visits
    row_map = lambda v, vb, ve, vn, nv: (vb[v], 0)
    exp_map = lambda v, vb, ve, vn, nv: (ve[v], 0, 0)
    grid_spec = pltpu.PrefetchScalarGridSpec(
        num_scalar_prefetch=4,
        grid=(vblk.shape[0],),
        in_specs=[
            pl.BlockSpec((bm, D), row_map),
            pl.BlockSpec((1, D, ff2), exp_map),
            pl.BlockSpec((1, 1, ff2), exp_map),
            pl.BlockSpec((1, ff2 // 2, D), exp_map),
            pl.BlockSpec((1, 1, D), exp_map),
        ],
        out_specs=pl.BlockSpec((bm, D), lambda v, vb, ve, vn, nv: (v, 0)),
        scratch_shapes=[pltpu.VMEM((D, ff2), BF16), pltpu.VMEM((ff2 // 2, D), BF16)],
    )
    return pl.pallas_call(
        _expert_kernel,
        grid_spec=grid_spec,
        out_shape=jax.ShapeDtypeStruct((vblk.shape[0] * bm, D), F32),
        compiler_params=pltpu.CompilerParams(
            dimension_semantics=("arbitrary",), vmem_limit_bytes=VMEM_LIMIT),
        name="moe_experts",
    )(vblk, vexp, vnew, nvis, xb, w_gu, b_gu.reshape(E, 1, ff2), w_dn,
      b_dn.reshape(E, 1, D))


def _visit_schedule(counts, n_rows, bm):
    n_blocks = n_rows // bm
    max_visits = n_blocks + N_EXPERTS - 1
    ends = jnp.cumsum(counts)
    starts = ends - counts
    first_blk = starts // bm
    last_blk = jnp.maximum(ends - 1, 0) // bm
    n_vis = jnp.where(counts > 0, last_blk - first_blk + 1, 0)
    vis_end = jnp.cumsum(n_vis)
    vis_start = vis_end - n_vis
    total = vis_end[-1]
    v = jnp.minimum(jnp.arange(max_visits, dtype=jnp.int32), total - 1)
    vexp = jnp.sum(v[:, None] >= vis_end[None, :], axis=-1).astype(jnp.int32)
    pick = lambda table: jnp.sum(
        jnp.where(vexp[:, None] == jnp.arange(N_EXPERTS, dtype=jnp.int32), table[None, :], 0), axis=-1)
    vblk = (pick(first_blk) + v - pick(vis_start)).astype(jnp.int32)
    vnew = jnp.concatenate([jnp.ones((1,), jnp.int32),
                            (vexp[1:] != vexp[:-1]).astype(jnp.int32)])
    tables = (vblk, vexp, vnew, total.astype(jnp.int32).reshape(1))
    return tables, (vis_start - first_blk).astype(jnp.int32)


def _combine_kernel(dest_ref, dest_next_ref, yb_ref, gate_ref, h_ref, gf_ref, out_ref,
                    buf_ref, sem):
    tm = h_ref.shape[0]
    step = pl.program_id(0)
    last = pl.num_programs(0) - 1

    def issue(idx_ref, slot):
        for t in range(tm):
            for kk in range(TOP_K):
                pltpu.make_async_copy(
                    yb_ref.at[pl.ds(idx_ref[kk, t], 1)],
                    buf_ref.at[slot, kk, pl.ds(t, 1)], sem.at[slot]
                ).start(priority=kk % 2)

    def drain(slot):
        for kk in range(TOP_K):
            pltpu.make_async_copy(yb_ref.at[pl.ds(0, tm)], buf_ref.at[slot, kk], sem.at[slot]).wait()

    def reduce(slot):
        drain(slot)
        gates = gate_ref[...]
        y = h_ref[...]
        for kk in range(TOP_K):
            y = y + gates[:, kk:kk + 1] * buf_ref[slot, kk]
        out_ref[...] = _rms(y, gf_ref[...])

    @pl.when(step == 0)
    def _():
        issue(dest_ref, 0)

    for slot in range(2):
        @pl.when(step % 2 == slot)
        def _():
            reduce(slot)
            issue(dest_next_ref, 1 - slot)

            @pl.when(step == last)
            def _():
                drain(1 - slot)


def _combine(yb, src, gates, h, final_g):
    T, D = h.shape
    tm = COMBINE_TILE
    n_steps = T // tm
    return pl.pallas_call(
        _combine_kernel,
        grid=(n_steps,),
        in_specs=[pl.BlockSpec((TOP_K, tm), lambda i: (0, i), memory_space=pltpu.SMEM),
                  pl.BlockSpec((TOP_K, tm), lambda i: (0, jnp.minimum(i + 1, n_steps - 1)),
                               memory_space=pltpu.SMEM),
                  pl.BlockSpec(memory_space=pl.ANY),
                  pl.BlockSpec((tm, TOP_K), lambda i: (i, 0)),
                  pl.BlockSpec((tm, D), lambda i: (i, 0)),
                  pl.BlockSpec((1, D), lambda i: (0, 0))],
        out_specs=pl.BlockSpec((tm, D), lambda i: (i, 0)),
        out_shape=jax.ShapeDtypeStruct((T, D), F32),
        scratch_shapes=[pltpu.VMEM((2, TOP_K, tm, D), F32), pltpu.SemaphoreType.DMA((2,))],
        compiler_params=pltpu.CompilerParams(
            dimension_semantics=("arbitrary",), vmem_limit_bytes=VMEM_LIMIT),
        name="moe_combine",
    )(src, src, yb, gates, h, final_g.reshape(1, D))


def _rope_tables(seq_len):
    half = HEAD_DIM // 2
    inv_freq = 1.0 / (ROPE_THETA ** (jnp.arange(0, HEAD_DIM, 2, dtype=F32) / HEAD_DIM))
    ang = jnp.arange(seq_len, dtype=F32)[:, None] * inv_freq[None, :]
    cos = jnp.tile(jnp.cos(ang), (1, LANES // half))
    sin = jnp.tile(jnp.concatenate([-jnp.sin(ang), jnp.sin(ang)], axis=-1), (1, LANES // HEAD_DIM))
    return cos, sin


def _moe(hn, h, idx, gates, rank, counts, w_gu, b_gu, w_dn, b_dn, final_g):
    T, D = hn.shape
    n_assign = T * TOP_K
    counts = counts.reshape(N_EXPERTS)
    starts = jnp.cumsum(counts) - counts
    onehot = idx[..., None] == jnp.arange(N_EXPERTS, dtype=jnp.int32)
    dest = rank + jnp.sum(jnp.where(onehot, starts, 0), axis=-1)
    bm = EXPERT_BLOCK
    visits, visit_offset = _visit_schedule(counts, n_assign, bm)
    visit = dest // bm + jnp.sum(jnp.where(onehot, visit_offset, 0), axis=-1)
    src = visit * bm + dest % bm

    xb = _dispatch(hn, dest.astype(jnp.int32), n_assign)
    yb = _experts(xb, visits, w_gu, b_gu, w_dn, b_dn)
    return _combine(yb, src.astype(jnp.int32), gates.T, h, final_g)


def kernel(x, norm1_g, w_mix_in, attn_norm_g, conv_w, conv_norm_g, w_mix_out, norm2_g,
           w_router, b_router, w_gate_up, b_gate_up, w_down, b_down, final_norm_g):
    B, S, D = x.shape
    assert norm1_g.shape[0] == 1, "single-layer block"
    cos_tab, sin_tab = _rope_tables(S)
    w_in = w_mix_in[0].astype(BF16)
    w_in = jnp.concatenate([w_in[:, 3 * ATT_WIDTH:], w_in[:, :3 * ATT_WIDTH]], axis=1)
    nat, d4, d16, conv_n = _in_projection(x, norm1_g[0], w_in, conv_w[0],
                                          conv_norm_g[0], cos_tab, sin_tab)
    attn = _dilated_attention(nat, d4, d16)

    T = B * S
    h, hn, idx, gates, rank, counts = _out_projection_router(
        attn.reshape(T, ATT_WIDTH), conv_n.reshape(T, CONV_WIDTH), x.reshape(T, D),
        attn_norm_g[0], w_mix_out[0].astype(BF16), norm2_g[0], w_router[0], b_router[0])
    out = _moe(hn, h, idx, gates, rank, counts, w_gate_up[0], b_gate_up[0], w_down[0],
               b_down[0], final_norm_g)
    return out.reshape(B, S, D)
```
